```python
import math
import jax, jax.numpy as jnp
from jax import lax
import numpy as np

D_MODEL = 2048
BATCH = 4
SEQ = 2048
DEPTH = 4
DEC_BATCH = 128
DEC_SEQ = 8
PAST_LEN = 8192
PAGE_SIZE = 128

EPS = 1e-6
BRANCH_W = D_MODEL // 2
N_BRANCH = 3
MLA_V = 128
MLA_HEADS = BRANCH_W // MLA_V
MLA_NOPE = 128
MLA_ROPE = 64
MLA_Q_RANK = 512
MLA_KV_RANK = 256
MLA_SCALE = (MLA_NOPE + MLA_ROPE) ** -0.5
ROPE_BASE = 10000.0
Q_BLOCK = 128
SSM_INNER = BRANCH_W
SSM_HEAD_DIM = 64
SSM_HEADS = SSM_INNER // SSM_HEAD_DIM
SSM_GROUPS = 2
SSM_STATE = 128
SSM_CONV = 4
SSM_CONV_DIM = SSM_INNER + 2 * SSM_GROUPS * SSM_STATE
SSM_CHUNK = 128
GLA_HEADS = 4
GLA_VALUE = BRANCH_W
GLA_KEY = GLA_VALUE // 2
GLA_DK = GLA_KEY // GLA_HEADS
GLA_DV = GLA_VALUE // GLA_HEADS
GLA_GATE_RANK = 16
GLA_GATE_NORMALIZER = 16.0
GLA_CHUNK = 16
IN_SPLITS = (MLA_Q_RANK, MLA_KV_RANK, MLA_ROPE,
             SSM_INNER, SSM_CONV_DIM, SSM_HEADS,
             GLA_KEY, GLA_KEY, GLA_VALUE, GLA_GATE_RANK, GLA_VALUE,
             N_BRANCH * D_MODEL)
IN_WIDTH = sum(IN_SPLITS)
PEER_HEADS = 8
PEER_NKEYS = 128
PEER_EXPERTS = PEER_NKEYS * PEER_NKEYS
PEER_DKEY = 256
PEER_TOPK = 16
PEER_BLOCK = 128

kernel_name = 'hybrid_mla_ssd_gla_peer_step'


def rmsnorm(x, g):
    x32 = x.astype(jnp.float32)
    y = x32 * lax.rsqrt(jnp.mean(x32 * x32, axis=-1, keepdims=True) + EPS)
    return (y * g.astype(jnp.float32)).astype(x.dtype)


def rope(x, pos):
    half = x.shape[-1] // 2
    inv = ROPE_BASE ** (-jnp.arange(half, dtype=jnp.float32) / half)
    ang = pos.astype(jnp.float32)[:, None] * inv[None, :]
    shp = (1, x.shape[1]) + (1,) * (x.ndim - 3) + (half,)
    cos, sin = jnp.cos(ang).reshape(shp), jnp.sin(ang).reshape(shp)
    x1 = x[..., :half].astype(jnp.float32)
    x2 = x[..., half:].astype(jnp.float32)
    return jnp.concatenate([x1 * cos - x2 * sin, x2 * cos + x1 * sin], axis=-1).astype(x.dtype)


def _chunk(a, L, pad):
    a = jnp.pad(a, [(0, 0), (0, pad)] + [(0, 0)] * (a.ndim - 2))
    return a.reshape((a.shape[0], a.shape[1] // L, L) + a.shape[2:])


def _chunk_scan(decay, states, s0):
    def step(s, inp):
        d, u = inp
        return d * s + u, s
    s_fin, s_in = lax.scan(step, s0, (jnp.moveaxis(decay, 1, 0), jnp.moveaxis(states, 1, 0)))
    return s_fin, jnp.moveaxis(s_in, 0, 1)


def causal_conv(x, buf, w, bias):
    xp = jnp.concatenate([buf.astype(x.dtype), x], axis=1)
    c = x.shape[-1]
    y = lax.conv_general_dilated(xp, w[:, None, :].astype(x.dtype), (1,), 'VALID',
                                 dimension_numbers=('NWC', 'WIO', 'NWC'), feature_group_count=c)
    return y + bias.astype(x.dtype), xp[:, xp.shape[1] - (SSM_CONV - 1):]


def mla_attend_prompt(q_lat, q_rope, ckv, kr):
    b, t, hh, r = q_lat.shape
    blk = Q_BLOCK if t % Q_BLOCK == 0 else t
    nb = t // blk
    k_pos = jnp.arange(t)

    def one_block(args):
        i, ql, qr = args
        s = (jnp.einsum('bqhr,bkr->bhqk', ql, ckv) + jnp.einsum('bqhd,bkd->bhqk', qr, kr)).astype(jnp.float32) * MLA_SCALE
        q_pos = i * blk + jnp.arange(blk)
        s = jnp.where(k_pos[None, :] <= q_pos[:, None], s, -jnp.inf)
        p = jax.nn.softmax(s, axis=-1).astype(ckv.dtype)
        return jnp.einsum('bhqk,bkr->bqhr', p, ckv)

    qlb = jnp.moveaxis(q_lat.reshape(b, nb, blk, hh, r), 1, 0)
    qrb = jnp.moveaxis(q_rope.reshape(b, nb, blk, hh, MLA_ROPE), 1, 0)
    out = lax.map(one_block, (jnp.arange(nb), qlb, qrb))
    return jnp.moveaxis(out, 0, 1).reshape(b, t, hh, r)


def mla_attend_cached(q_lat, q_rope, ckv, kr, past_lat, past_kr):
    t = q_lat.shape[1]
    n_past = past_lat.shape[1]
    s_past = (jnp.einsum('bqhr,bkr->bhqk', q_lat, past_lat) + jnp.einsum('bqhd,bkd->bhqk', q_rope, past_kr)).astype(jnp.float32) * MLA_SCALE
    s_new = (jnp.einsum('bqhr,bkr->bhqk', q_lat, ckv) + jnp.einsum('bqhd,bkd->bhqk', q_rope, kr)).astype(jnp.float32) * MLA_SCALE
    s_new = jnp.where(jnp.tril(jnp.ones((t, t), bool)), s_new, -jnp.inf)
    p = jax.nn.softmax(jnp.concatenate([s_past, s_new], axis=-1), axis=-1).astype(ckv.dtype)
    return (jnp.einsum('bhqk,bkr->bqhr', p[..., :n_past], past_lat)
            + jnp.einsum('bhqk,bkr->bqhr', p[..., n_past:], ckv))


def mla_branch(cq, ckv, kr, pos, past, W):
    b, t, _ = cq.shape
    q = jnp.einsum('btr,rhd->bthd', rmsnorm(cq, W['mla_q_norm']), W['mla_w_uq'])
    q_nope, q_rope = q[..., :MLA_NOPE], rope(q[..., MLA_NOPE:], pos)
    ckv = rmsnorm(ckv, W['mla_kv_norm'])
    kr = rope(kr, pos)
    q_lat = jnp.einsum('bthd,rhd->bthr', q_nope, W['mla_w_uk'])
    if past is None:
        o_lat = mla_attend_prompt(q_lat, q_rope, ckv, kr)
    else:
        o_lat = mla_attend_cached(q_lat, q_rope, ckv, kr, past[0], past[1])
    o = jnp.einsum('bthr,rhv->bthv', o_lat, W['mla_w_uv']).reshape(b, t, MLA_HEADS * MLA_V)
    return o, ckv, kr


def ssd_chunked(xdt, da, bm, cm, s0):
    b, t, hh, pp = xdt.shape
    g, n = bm.shape[2], bm.shape[3]
    hg = hh // g
    L = min(SSM_CHUNK, t)
    pad = (-t) % L
    nc = (t + pad) // L
    xc = _chunk(xdt.reshape(b, t, g, hg, pp), L, pad)
    ac = _chunk(da.reshape(b, t, g, hg), L, pad)
    bc, cc = _chunk(bm, L, pad), _chunk(cm, L, pad)
    acs = jnp.cumsum(ac, axis=2)
    causal = jnp.tril(jnp.ones((L, L), bool))[None, None, :, :, None, None]
    seg = jnp.exp(jnp.where(causal, acs[:, :, :, None] - acs[:, :, None, :], -jnp.inf))
    cb = jnp.einsum('bcign,bcjgn->bcijg', cc, bc)
    y = jnp.einsum('bcijgh,bcjghp->bcighp', cb[..., None] * seg, xc)
    decay_to_end = jnp.exp(acs[:, :, -1:] - acs)
    states = jnp.einsum('bcjgn,bcjgh,bcjghp->bcghpn', bc, decay_to_end, xc)
    s_fin, s_in = _chunk_scan(jnp.exp(acs[:, :, -1])[..., None, None], states,
                              s0.reshape(b, g, hg, pp, n))
    y = y + jnp.einsum('bcign,bcghpn->bcighp', cc, s_in) * jnp.exp(acs)[..., None]
    return y.reshape(b, nc * L, hh, pp)[:, :t], s_fin.reshape(b, hh, pp, n)


def mamba_branch(z, xbc, dt, conv_buf, s0, W):
    b, t, _ = z.shape
    f32 = jnp.float32
    xbc, conv_new = causal_conv(xbc, conv_buf, W['ssm_conv_w'], W['ssm_conv_b'])
    xbc = jax.nn.silu(xbc).astype(f32)
    xs = xbc[..., :SSM_INNER].reshape(b, t, SSM_HEADS, SSM_HEAD_DIM)
    bm = xbc[..., SSM_INNER:SSM_INNER + SSM_GROUPS * SSM_STATE].reshape(b, t, SSM_GROUPS, SSM_STATE)
    cm = xbc[..., SSM_INNER + SSM_GROUPS * SSM_STATE:].reshape(b, t, SSM_GROUPS, SSM_STATE)
    dt = jax.nn.softplus(dt.astype(f32) + W['ssm_dt_bias'].astype(f32))
    a = -jnp.exp(W['ssm_a_log'].astype(f32))
    y, s_new = ssd_chunked(xs * dt[..., None], dt * a, bm, cm, s0.astype(f32))
    y = y + W['ssm_d'].astype(f32)[:, None] * xs
    y = y.reshape(b, t, SSM_INNER) * jax.nn.silu(z.astype(f32))
    yg = y.reshape(b, t, SSM_GROUPS, SSM_INNER // SSM_GROUPS)
    yg = yg * lax.rsqrt(jnp.mean(yg * yg, axis=-1, keepdims=True) + EPS)
    y = yg.reshape(b, t, SSM_INNER) * W['ssm_norm'].astype(f32)
    return y.astype(z.dtype), conv_new, s_new.astype(z.dtype)


def gla_chunked(q, k, v, gk, s0):
    b, t, hh, dk = q.shape
    L = min(GLA_CHUNK, t)
    pad = (-t) % L
    nc = (t + pad) // L
    qc, kc, vc, gc = [_chunk(a, L, pad) for a in (q, k, v, gk)]
    bcs = jnp.cumsum(gc, axis=2)
    causal = jnp.tril(jnp.ones((L, L), bool))[None, None, :, :, None, None]
    decay = jnp.exp(jnp.where(causal, bcs[:, :, :, None] - bcs[:, :, None, :], -jnp.inf))
    attn = jnp.sum(qc[:, :, :, None] * kc[:, :, None, :] * decay, axis=-1)
    o = jnp.einsum('bcijh,bcjhv->bcihv', attn, vc)
    k_end = kc * jnp.exp(bcs[:, :, -1:] - bcs)
    states = jnp.einsum('bcjhd,bcjhv->bchdv', k_end, vc)
    s_fin, s_in = _chunk_scan(jnp.exp(bcs[:, :, -1])[..., None], states, s0)
    o = o + jnp.einsum('bcihd,bchdv->bcihv', qc * jnp.exp(bcs), s_in)
    return o.reshape(b, nc * L, hh, -1)[:, :t], s_fin


def gla_branch(q, k, v, g_lr, og, s0, W):
    b, t, _ = q.shape
    f32 = jnp.float32
    gk = jax.nn.log_sigmoid((g_lr @ W['gla_gate_w'] + W['gla_gate_b']).astype(f32)) / GLA_GATE_NORMALIZER
    q = q.reshape(b, t, GLA_HEADS, GLA_DK).astype(f32) * (GLA_DK ** -0.5)
    k = k.reshape(b, t, GLA_HEADS, GLA_DK).astype(f32)
    v = v.reshape(b, t, GLA_HEADS, GLA_DV).astype(f32)
    o, s_new = gla_chunked(q, k, v, gk.reshape(b, t, GLA_HEADS, GLA_DK), s0.astype(f32))
    o = o * lax.rsqrt(jnp.mean(o * o, axis=-1, keepdims=True) + EPS) * W['gla_norm'].astype(f32)
    o = o.reshape(b, t, GLA_VALUE) * jax.nn.silu(og.astype(f32))
    return o.astype(og.dtype), s_new.astype(og.dtype)


def peer(h, W):
    b, t, d = h.shape
    n = b * t
    hf = h.reshape(n, d)
    q = (hf @ W['peer_wq']).reshape(n, PEER_HEADS, 2, PEER_DKEY // 2)
    s = jnp.einsum('nhpd,phkd->nhpk', q, W['peer_subkeys']).astype(jnp.float32)
    s1, i1 = lax.top_k(s[:, :, 0], PEER_TOPK)
    s2, i2 = lax.top_k(s[:, :, 1], PEER_TOPK)
    cand = (s1[..., :, None] + s2[..., None, :]).reshape(n, PEER_HEADS, PEER_TOPK * PEER_TOPK)
    sc, ci = lax.top_k(cand, PEER_TOPK)
    idx = (jnp.take_along_axis(i1, ci // PEER_TOPK, axis=-1) * PEER_NKEYS
           + jnp.take_along_axis(i2, ci % PEER_TOPK, axis=-1))
    gate = jax.nn.softmax(sc, axis=-1).astype(h.dtype)
    kk = PEER_HEADS * PEER_TOPK
    blk = min(PEER_BLOCK, n)
    pad = (-n) % blk
    nb = (n + pad) // blk

    def blocks(a):
        return jnp.pad(a, [(0, pad)] + [(0, 0)] * (a.ndim - 1)).reshape((nb, blk) + a.shape[1:])

    def one_block(args):
        hb, ib, gb = args
        u = jnp.take(W['peer_u'], ib, axis=0)
        act = jax.nn.gelu(jnp.einsum('nd,nkd->nk', hb, u), approximate=False)
        return jnp.einsum('nk,nkd->nd', gb * act, jnp.take(W['peer_v'], ib, axis=0))

    out = lax.map(one_block, (blocks(hf), blocks(idx.reshape(n, kk)), blocks(gate.reshape(n, kk))))
    return out.reshape(nb * blk, d)[:n].reshape(b, t, d)


def hybrid_layer(x, pos, mla_past, conv_buf, ssm_s0, gla_s0, W):
    b, t, _ = x.shape
    h = rmsnorm(x, W['norm_mix'])
    p = h @ W['w_in']
    (cq, ckv, kr, z, xbc, dt, gq, gk, gv, g_lr, og, gates) = jnp.split(p, np.cumsum(IN_SPLITS)[:-1].tolist(), axis=-1)
    o_a, ckv_n, kr_n = mla_branch(cq, ckv, kr, pos, mla_past, W)
    o_b, conv_new, ssm_new = mamba_branch(z, xbc, dt, conv_buf, ssm_s0, W)
    o_c, gla_new = gla_branch(gq, gk, gv, g_lr, og, gla_s0, W)
    br = jnp.einsum('btiw,iwd->btid', jnp.stack([o_a, o_b, o_c], axis=2), W['w_branch'])
    g = jax.nn.sigmoid(gates.reshape(b, t, N_BRANCH, D_MODEL))
    x = x + jnp.sum(g * br, axis=2) @ W['w_out']
    x = x + peer(rmsnorm(x, W['norm_ffn']), W)
    return x, (ckv_n, kr_n, conv_new, ssm_new, gla_new)


def setup_inputs(seed: int = 0) -> dict:
    key = jax.random.key(seed)
    k = jax.random.split(key, 32)
    f32 = jnp.float32

    def nrm(i, shape, scale):
        return jax.random.normal(k[i], shape, f32) * scale

    n_pages = PAST_LEN // PAGE_SIZE
    used = DEC_BATCH * n_pages
    n_phys = used + max(1, used // 4)
    page_table = jax.random.permutation(k[0], n_phys)[:used].reshape(DEC_BATCH, n_pages).astype(jnp.int32)
    dt0 = jnp.exp(jax.random.uniform(k[1], (DEPTH, SSM_HEADS), f32, math.log(1e-3), math.log(1e-1)))
    dt_bias = dt0 + jnp.log(-jnp.expm1(-dt0))
    a_log = jnp.log(jax.random.uniform(k[2], (DEPTH, SSM_HEADS), f32, 1.0, 16.0))
    ones_noise = lambda i, shape: 1.0 + nrm(i, shape, 0.05)
    return {
        'x_prompt': nrm(3, (BATCH, SEQ, D_MODEL), 1.0),
        'x_sample': nrm(4, (DEC_BATCH, DEC_SEQ, D_MODEL), 1.0),
        'cache_mla_latent': nrm(5, (DEPTH, n_phys, PAGE_SIZE, MLA_KV_RANK), 1.0),
        'cache_mla_krope': nrm(6, (DEPTH, n_phys, PAGE_SIZE, MLA_ROPE), 1.0),
        'page_table': page_table,
        'state_ssm_conv': nrm(7, (DEPTH, DEC_BATCH, SSM_CONV - 1, SSM_CONV_DIM), 1.0),
        'state_ssm': nrm(8, (DEPTH, DEC_BATCH, SSM_HEADS, SSM_HEAD_DIM, SSM_STATE), 0.1),
        'state_gla': nrm(9, (DEPTH, DEC_BATCH, GLA_HEADS, GLA_DK, GLA_DV), 0.5),
        'norm_mix': ones_noise(10, (DEPTH, D_MODEL)),
        'w_in': nrm(11, (DEPTH, D_MODEL, IN_WIDTH), D_MODEL ** -0.5),
        'mla_q_norm': ones_noise(12, (DEPTH, MLA_Q_RANK)),
        'mla_w_uq': nrm(13, (DEPTH, MLA_Q_RANK, MLA_HEADS, MLA_NOPE + MLA_ROPE), MLA_Q_RANK ** -0.5),
        'mla_kv_norm': ones_noise(14, (DEPTH, MLA_KV_RANK)),
        'mla_w_uk': nrm(15, (DEPTH, MLA_KV_RANK, MLA_HEADS, MLA_NOPE), MLA_KV_RANK ** -0.5),
        'mla_w_uv': nrm(16, (DEPTH, MLA_KV_RANK, MLA_HEADS, MLA_V), MLA_KV_RANK ** -0.5),
        'ssm_conv_w': nrm(17, (DEPTH, SSM_CONV, SSM_CONV_DIM), SSM_CONV ** -0.5),
        'ssm_conv_b': nrm(18, (DEPTH, SSM_CONV_DIM), 0.02),
        'ssm_dt_bias': dt_bias,
        'ssm_a_log': a_log,
        'ssm_d': 1.0 + nrm(19, (DEPTH, SSM_HEADS), 0.1),
        'ssm_norm': ones_noise(20, (DEPTH, SSM_INNER)),
        'gla_gate_w': nrm(21, (DEPTH, GLA_GATE_RANK, GLA_KEY), GLA_GATE_RANK ** -0.5),
        'gla_gate_b': nrm(22, (DEPTH, GLA_KEY), 0.02),
        'gla_norm': ones_noise(23, (DEPTH, GLA_DV)),
        'w_branch': nrm(24, (DEPTH, N_BRANCH, BRANCH_W, D_MODEL), BRANCH_W ** -0.5),
        'w_out': nrm(25, (DEPTH, D_MODEL, D_MODEL), D_MODEL ** -0.5),
        'norm_ffn': ones_noise(26, (DEPTH, D_MODEL)),
        'peer_wq': nrm(27, (DEPTH, D_MODEL, PEER_HEADS * PEER_DKEY), D_MODEL ** -0.5),
        'peer_subkeys': nrm(28, (DEPTH, 2, PEER_HEADS, PEER_NKEYS, PEER_DKEY // 2), (PEER_DKEY // 2) ** -0.5),
        'peer_u': nrm(29, (DEPTH, PEER_EXPERTS, D_MODEL), D_MODEL ** -0.5),
        'peer_v': nrm(30, (DEPTH, PEER_EXPERTS, D_MODEL), (PEER_HEADS * PEER_TOPK) ** -0.5),
        'norm_final': ones_noise(31, (D_MODEL,)),
    }


def reference(x_prompt, x_sample, cache_mla_latent, cache_mla_krope, page_table, state_ssm_conv,
              state_ssm, state_gla, norm_mix, w_in, mla_q_norm, mla_w_uq, mla_kv_norm, mla_w_uk,
              mla_w_uv, ssm_conv_w, ssm_conv_b, ssm_dt_bias, ssm_a_log, ssm_d, ssm_norm, gla_gate_w,
              gla_gate_b, gla_norm, w_branch, w_out, norm_ffn, peer_wq, peer_subkeys, peer_u, peer_v,
              norm_final):
    bp, tp, _ = x_prompt.shape
    ts = x_sample.shape[1]
    n_seq, n_pages = page_table.shape
    past_len = n_pages * cache_mla_latent.shape[2]
    pos_p = jnp.arange(tp)
    pos_s = past_len + jnp.arange(ts)
    dt_ = x_prompt.dtype
    conv0 = jnp.zeros((bp, SSM_CONV - 1, SSM_CONV_DIM), dt_)
    ssm0 = jnp.zeros((bp, SSM_HEADS, SSM_HEAD_DIM, SSM_STATE), dt_)
    gla0 = jnp.zeros((bp, GLA_HEADS, GLA_DK, GLA_DV), dt_)
    xp, xs = x_prompt, x_sample
    new_p = [[], [], [], [], []]
    new_s = [[], [], [], [], []]
    for l in range(DEPTH):
        W = dict(norm_mix=norm_mix[l], w_in=w_in[l], mla_q_norm=mla_q_norm[l], mla_w_uq=mla_w_uq[l],
                 mla_kv_norm=mla_kv_norm[l], mla_w_uk=mla_w_uk[l], mla_w_uv=mla_w_uv[l],
                 ssm_conv_w=ssm_conv_w[l], ssm_conv_b=ssm_conv_b[l], ssm_dt_bias=ssm_dt_bias[l],
                 ssm_a_log=ssm_a_log[l], ssm_d=ssm_d[l], ssm_norm=ssm_norm[l],
                 gla_gate_w=gla_gate_w[l], gla_gate_b=gla_gate_b[l], gla_norm=gla_norm[l],
                 w_branch=w_branch[l], w_out=w_out[l], norm_ffn=norm_ffn[l], peer_wq=peer_wq[l],
                 peer_subkeys=peer_subkeys[l], peer_u=peer_u[l], peer_v=peer_v[l])
        xp, st_p = hybrid_layer(xp, pos_p, None, conv0, ssm0, gla0, W)
        past = (cache_mla_latent[l][page_table].reshape(n_seq, past_len, MLA_KV_RANK),
                cache_mla_krope[l][page_table].reshape(n_seq, past_len, MLA_ROPE))
        xs, st_s = hybrid_layer(xs, pos_s, past, state_ssm_conv[l], state_ssm[l], state_gla[l], W)
        for lst, a in zip(new_p, st_p):
            lst.append(a)
        for lst, a in zip(new_s, st_s):
            lst.append(a)
    y_prompt = rmsnorm(xp, norm_final)
    y_sample = rmsnorm(xs, norm_final)
    lat_p, krope_p, conv_p, ssm_p, gla_p = [jnp.stack(a) for a in new_p]
    lat_s, krope_s, conv_s, ssm_s, gla_s = [jnp.stack(a) for a in new_s]
    return (y_prompt, y_sample, lat_p, krope_p, conv_p, ssm_p, gla_p, lat_s, krope_s, conv_s, ssm_s, gla_s)
```

```python
import functools
import math

import jax
import jax.numpy as jnp
from jax import lax
from jax.experimental import pallas as pl
from jax.experimental.pallas import tpu as pltpu

F32 = jnp.float32
BF16 = jnp.bfloat16
HI = lax.Precision.HIGHEST
NEG_INF = float("-inf")

EPS = 1e-6
D_MODEL = 2048
BRANCH_W = 1024
MLA_HEADS = 8
MLA_NOPE = 128
MLA_ROPE = 64
MLA_V = 128
MLA_Q_RANK = 512
MLA_KV_RANK = 256
MLA_SCALE = (MLA_NOPE + MLA_ROPE) ** -0.5
ROPE_BASE = 10000.0
SSM_INNER = 1024
SSM_HEAD_DIM = 64
SSM_HEADS = 16
SSM_GROUPS = 2
SSM_STATE = 128
SSM_CONV = 4
SSM_CONV_DIM = SSM_INNER + 2 * SSM_GROUPS * SSM_STATE
SSM_CHUNK = 128
GLA_HEADS = 4
GLA_DK = 128
GLA_DV = 256
GLA_KEY = 512
GLA_VALUE = 1024
GLA_GATE_RANK = 16
GLA_GATE_NORMALIZER = 16.0
GLA_SUB = 16
PEER_HEADS = 8
PEER_NKEYS = 128
PEER_TOPK = 16
PEER_HALF = 128

LANE = 128
VMEM_LIMIT = 56 * 1024 * 1024

COL_GATES = 0
COL_Z = 6144
COL_OG = 7168
COL_GV = 8192
COL_XBC = 9216
COL_GQ = 10752
COL_GK = 11264
COL_CQ = 11776
COL_CKV = 12288
COL_MISC = 12544
IN_COLS = 12800
MISC_W = 256


def _tile(n, pref, mult=8):
    for t in range(min(n, pref), 0, -1):
        if n % t == 0 and t % mult == 0:
            return t
    return n


def _params(*sem):
    return pltpu.CompilerParams(dimension_semantics=sem, vmem_limit_bytes=VMEM_LIMIT)


def _rms(x, g):
    return x * lax.rsqrt(jnp.mean(x * x, axis=-1, keepdims=True) + EPS) * g


def _dot(a, b, prec=None):
    return jnp.dot(a, b, preferred_element_type=F32, precision=prec)


def _dot_nt(a, b, prec=None):
    return lax.dot_general(a, b, (((1,), (1,)), ((), ())), preferred_element_type=F32, precision=prec)


def _dot_tn(a, b, prec=None):
    return lax.dot_general(a, b, (((0,), (0,)), ((), ())), preferred_element_type=F32, precision=prec)


def _silu(x):
    return x * jax.nn.sigmoid(x)


def _norm_matmul_kernel(x_ref, g_ref, w_ref, o_ref, h_ref):
    @pl.when(pl.program_id(1) == 0)
    def _():
        h_ref[...] = _rms(x_ref[...], g_ref[...]).astype(BF16)

    o_ref[...] = _dot(h_ref[...], w_ref[...])


def _norm_matmul(x, g, w, tm_pref=1024, tn=512):
    m, k = x.shape
    n = w.shape[1]
    tm = _tile(m, tm_pref)
    return pl.pallas_call(
        _norm_matmul_kernel,
        out_shape=jax.ShapeDtypeStruct((m, n), F32),
        grid=(m // tm, n // tn),
        in_specs=[pl.BlockSpec((tm, k), lambda i, j: (i, 0)),
                  pl.BlockSpec((1, k), lambda i, j: (0, 0)),
                  pl.BlockSpec((k, tn), lambda i, j: (0, j))],
        out_specs=pl.BlockSpec((tm, tn), lambda i, j: (i, j)),
        scratch_shapes=[pltpu.VMEM((tm, k), BF16)],
        compiler_params=_params("arbitrary", "arbitrary"),
        name="norm_in_proj",
    )(x, g, w)


def _mla_prep_kernel(cq_ref, ckv_ref, misc_ref, cos_ref, sin_ref, qn_ref, kvn_ref, wq_ref, wuk_ref,
                     ql_ref, qr_ref, lat_ref, kr_ref):
    cqn = _rms(cq_ref[...], qn_ref[...]).astype(BF16)
    q = _dot(cqn, wq_ref[...])
    cos = cos_ref[...]
    sin = sin_ref[...]
    nope_w = MLA_HEADS * MLA_NOPE
    rope_w = MLA_HEADS * MLA_ROPE
    qr_ref[...] = (q[:, nope_w:nope_w + rope_w] * cos + q[:, nope_w + rope_w:] * sin) * MLA_SCALE
    for h in range(MLA_HEADS):
        qh = q[:, h * MLA_NOPE:(h + 1) * MLA_NOPE].astype(BF16)
        ql_ref[:, h * MLA_KV_RANK:(h + 1) * MLA_KV_RANK] = _dot(qh, wuk_ref[h]) * MLA_SCALE
    lat_ref[...] = _rms(ckv_ref[...], kvn_ref[...])
    misc = misc_ref[...]
    kr_ref[...] = misc[:, 0:MLA_ROPE] * cos[:, 0:MLA_ROPE] + misc[:, MLA_ROPE:2 * MLA_ROPE] * sin[:, 0:MLA_ROPE]


def _mla_prep(p, cos, sin, qn, kvn, wq, wuk, tm_pref=256):
    rows = p.shape[0]
    tm = _tile(rows, tm_pref)
    lat_w = MLA_HEADS * MLA_KV_RANK
    rope_w = MLA_HEADS * MLA_ROPE
    return pl.pallas_call(
        _mla_prep_kernel,
        out_shape=[jax.ShapeDtypeStruct((rows, lat_w), F32),
                   jax.ShapeDtypeStruct((rows, rope_w), F32),
                   jax.ShapeDtypeStruct((rows, MLA_KV_RANK), F32),
                   jax.ShapeDtypeStruct((rows, MLA_ROPE), F32)],
        grid=(rows // tm,),
        in_specs=[pl.BlockSpec((tm, MLA_Q_RANK), lambda i: (i, COL_CQ // MLA_Q_RANK)),
                  pl.BlockSpec((tm, MLA_KV_RANK), lambda i: (i, COL_CKV // MLA_KV_RANK)),
                  pl.BlockSpec((tm, MISC_W), lambda i: (i, COL_MISC // MISC_W)),
                  pl.BlockSpec((tm, rope_w), lambda i: (i, 0)),
                  pl.BlockSpec((tm, rope_w), lambda i: (i, 0)),
                  pl.BlockSpec((1, MLA_Q_RANK), lambda i: (0, 0)),
                  pl.BlockSpec((1, MLA_KV_RANK), lambda i: (0, 0)),
                  pl.BlockSpec(wq.shape, lambda i: (0, 0)),
                  pl.BlockSpec(wuk.shape, lambda i: (0, 0, 0))],
        out_specs=[pl.BlockSpec((tm, lat_w), lambda i: (i, 0)),
                   pl.BlockSpec((tm, rope_w), lambda i: (i, 0)),
                   pl.BlockSpec((tm, MLA_KV_RANK), lambda i: (i, 0)),
                   pl.BlockSpec((tm, MLA_ROPE), lambda i: (i, 0))],
        compiler_params=_params("arbitrary"),
        name="mla_prep",
    )(p, p, p, cos, sin, qn, kvn, wq, wuk)


def _attn_prompt_kernel(ql_ref, qr_ref, k_ref, kr_ref, wuv_ref, o_ref, qlb, qrb, m_ref, l_ref, acc_ref, *, tq, tk):
    qi = pl.program_id(1)
    ki = pl.program_id(2)

    @pl.when(ki == 0)
    def _():
        qlb[...] = ql_ref[...].astype(BF16)
        qrb[...] = qr_ref[...].astype(BF16)
        m_ref[...] = jnp.full(m_ref.shape, NEG_INF, F32)
        l_ref[...] = jnp.zeros(l_ref.shape, F32)
        acc_ref[...] = jnp.zeros(acc_ref.shape, F32)

    @pl.when(ki <= qi)
    def _():
        k = k_ref[...].astype(BF16)
        kr = kr_ref[...].astype(BF16)
        row = qi * tq + lax.broadcasted_iota(jnp.int32, (tq, tk), 0)
        col = ki * tk + lax.broadcasted_iota(jnp.int32, (tq, tk), 1)
        mask = col <= row
        for h in range(MLA_HEADS):
            s = (_dot_nt(qlb[:, h * MLA_KV_RANK:(h + 1) * MLA_KV_RANK], k)
                 + _dot_nt(qrb[:, h * MLA_ROPE:(h + 1) * MLA_ROPE], kr))
            s = jnp.where(mask, s, NEG_INF)
            m_prev = m_ref[h]
            m_new = jnp.maximum(m_prev, jnp.max(s, axis=-1, keepdims=True))
            p = jnp.exp(s - m_new)
            alpha = jnp.exp(m_prev - m_new)
            l_ref[h] = alpha * l_ref[h] + jnp.sum(p, axis=-1, keepdims=True)
            acc_ref[h] = alpha * acc_ref[h] + _dot(p.astype(BF16), k)
            m_ref[h] = m_new

    @pl.when(ki == qi)
    def _():
        for h in range(MLA_HEADS):
            o = (acc_ref[h] / l_ref[h]).astype(BF16)
            o_ref[:, h * MLA_V:(h + 1) * MLA_V] = _dot(o, wuv_ref[h])


def _attn_prompt(ql, qr, lat, kr, wuv, nb, t, tq_pref=256):
    tq = _tile(t, tq_pref)
    nq = t // tq
    kern = functools.partial(_attn_prompt_kernel, tq=tq, tk=tq)
    return pl.pallas_call(
        kern,
        out_shape=jax.ShapeDtypeStruct((nb * t, MLA_HEADS * MLA_V), F32),
        grid=(nb, nq, nq),
        in_specs=[pl.BlockSpec((tq, MLA_HEADS * MLA_KV_RANK), lambda b, qi, ki: (b * nq + qi, 0)),
                  pl.BlockSpec((tq, MLA_HEADS * MLA_ROPE), lambda b, qi, ki: (b * nq + qi, 0)),
                  pl.BlockSpec((tq, MLA_KV_RANK), lambda b, qi, ki: (b * nq + jnp.minimum(ki, qi), 0)),
                  pl.BlockSpec((tq, MLA_ROPE), lambda b, qi, ki: (b * nq + jnp.minimum(ki, qi), 0)),
                  pl.BlockSpec(wuv.shape, lambda b, qi, ki: (0, 0, 0))],
        out_specs=pl.BlockSpec((tq, MLA_HEADS * MLA_V), lambda b, qi, ki: (b * nq + qi, 0)),
        scratch_shapes=[pltpu.VMEM((tq, MLA_HEADS * MLA_KV_RANK), BF16),
                        pltpu.VMEM((tq, MLA_HEADS * MLA_ROPE), BF16),
                        pltpu.VMEM((MLA_HEADS, tq, 1), F32),
                        pltpu.VMEM((MLA_HEADS, tq, 1), F32),
                        pltpu.VMEM((MLA_HEADS, tq, MLA_KV_RANK), F32)],
        compiler_params=_params("arbitrary", "arbitrary", "arbitrary"),
        name="attn_prompt",
    )(ql, qr, lat, kr, wuv)


def _attn_sample_kernel(pt_ref, ql_ref, qr_ref, kn_ref, krn_ref, wuv_ref, *rest, pages, npg, ts):
    lat_refs = rest[:pages]
    kr_refs = rest[pages:2 * pages]
    o_ref = rest[2 * pages]
    qs, qrs, knp, krnp, m_ref, l_ref, acc_ref = rest[2 * pages + 1:]
    j = pl.program_id(1)
    nrow = MLA_HEADS * ts

    @pl.when(j == 0)
    def _():
        for h in range(MLA_HEADS):
            qs[h * ts:(h + 1) * ts, :] = ql_ref[:, h * MLA_KV_RANK:(h + 1) * MLA_KV_RANK]
            qrs[h * ts:(h + 1) * ts, :] = qr_ref[:, h * MLA_ROPE:(h + 1) * MLA_ROPE]
        knp[...] = jnp.zeros(knp.shape, F32)
        krnp[...] = jnp.zeros(krnp.shape, F32)
        knp[0:ts, :] = kn_ref[...]
        krnp[0:ts, :] = krn_ref[...]
        m_ref[...] = jnp.full(m_ref.shape, NEG_INF, F32)
        l_ref[...] = jnp.zeros(l_ref.shape, F32)
        acc_ref[...] = jnp.zeros(acc_ref.shape, F32)

    qb = qs[...].astype(BF16)
    qrb = qrs[...].astype(BF16)

    def update(k, kr, mask):
        s = _dot_nt(qb, k) + _dot_nt(qrb, kr)
        if mask is not None:
            s = jnp.where(mask, s, NEG_INF)
        m_prev = m_ref[...]
        m_new = jnp.maximum(m_prev, jnp.max(s, axis=-1, keepdims=True))
        p = jnp.exp(s - m_new)
        alpha = jnp.exp(m_prev - m_new)
        l_ref[...] = alpha * l_ref[...] + jnp.sum(p, axis=-1, keepdims=True)
        acc_ref[...] = alpha * acc_ref[...] + _dot(p.astype(BF16), k)
        m_ref[...] = m_new

    for u in range(pages):
        update(lat_refs[u][...].astype(BF16), kr_refs[u][...].astype(BF16), None)

    @pl.when(j == npg - 1)
    def _():
        npad = knp.shape[0]
        tok = lax.broadcasted_iota(jnp.int32, (nrow, npad), 0) % ts
        col = lax.broadcasted_iota(jnp.int32, (nrow, npad), 1)
        update(knp[...].astype(BF16), krnp[...].astype(BF16), col <= tok)
        o = acc_ref[...] / l_ref[...]
        for h in range(MLA_HEADS):
            oh = o[h * ts:(h + 1) * ts, :].astype(BF16)
            o_ref[:, h * MLA_V:(h + 1) * MLA_V] = _dot(oh, wuv_ref[h])


def _attn_sample(ql, qr, lat, kr, cache_lat, cache_kr, page_table, layer, wuv, row0, nseq, ts, pages_pref=8):
    n_pages = page_table.shape[1]
    page = cache_lat.shape[2]
    pages = _tile(n_pages, pages_pref, 1)
    npg = n_pages // pages
    blk0 = row0 // ts
    nrow = MLA_HEADS * ts
    kern = functools.partial(_attn_sample_kernel, pages=pages, npg=npg, ts=ts)

    def page_map(u):
        return lambda b, j, pt: (layer, pt[b, j * pages + u], 0, 0)

    in_specs = [pl.BlockSpec((ts, MLA_HEADS * MLA_KV_RANK), lambda b, j, pt: (blk0 + b, 0)),
                pl.BlockSpec((ts, MLA_HEADS * MLA_ROPE), lambda b, j, pt: (blk0 + b, 0)),
                pl.BlockSpec((ts, MLA_KV_RANK), lambda b, j, pt: (blk0 + b, 0)),
                pl.BlockSpec((ts, MLA_ROPE), lambda b, j, pt: (blk0 + b, 0)),
                pl.BlockSpec(wuv.shape, lambda b, j, pt: (0, 0, 0))]
    in_specs += [pl.BlockSpec((None, None, page, MLA_KV_RANK), page_map(u)) for u in range(pages)]
    in_specs += [pl.BlockSpec((None, None, page, MLA_ROPE), page_map(u)) for u in range(pages)]
    grid_spec = pltpu.PrefetchScalarGridSpec(
        num_scalar_prefetch=1,
        grid=(nseq, npg),
        in_specs=in_specs,
        out_specs=pl.BlockSpec((ts, MLA_HEADS * MLA_V), lambda b, j, pt: (b, 0)),
        scratch_shapes=[pltpu.VMEM((nrow, MLA_KV_RANK), F32),
                        pltpu.VMEM((nrow, MLA_ROPE), F32),
                        pltpu.VMEM((LANE, MLA_KV_RANK), F32),
                        pltpu.VMEM((LANE, MLA_ROPE), F32),
                        pltpu.VMEM((nrow, 1), F32),
                        pltpu.VMEM((nrow, 1), F32),
                        pltpu.VMEM((nrow, MLA_KV_RANK), F32)])
    return pl.pallas_call(
        kern,
        out_shape=jax.ShapeDtypeStruct((nseq * ts, MLA_HEADS * MLA_V), F32),
        grid_spec=grid_spec,
        compiler_params=_params("arbitrary", "arbitrary"),
        name="attn_sample",
    )(page_table, ql, qr, lat, kr, wuv, *([cache_lat] * pages), *([cache_kr] * pages))


def _mamba_kernel(z_ref, xbc_ref, misc_ref, cbuf_ref, s0_ref, cw_ref, cbias_ref, dtb_ref, alog_ref, dpar_ref,
                  nw_ref, y_ref, sfin_ref, xp_ref, s_ref, ysc_ref, *, L, nc):
    c = pl.program_id(1)
    pad = 8

    @pl.when(c == 0)
    def _():
        xp_ref[pad - 3:pad, :] = cbuf_ref[...]
        s_ref[...] = s0_ref[...]

    x = xbc_ref[...]
    xp_ref[pad:pad + L, :] = x
    cw = cw_ref[...]
    y = (cbias_ref[...] + cw[3:4] * x + cw[2:3] * xp_ref[pad - 1:pad - 1 + L, :]
         + cw[1:2] * xp_ref[pad - 2:pad - 2 + L, :] + cw[0:1] * xp_ref[pad - 3:pad - 3 + L, :])
    tail = xp_ref[pad + L - 3:pad + L, :]
    xp_ref[pad - 3:pad, :] = tail
    xa = _silu(y)
    bm = xa[:, SSM_INNER:SSM_INNER + SSM_GROUPS * SSM_STATE]
    cm = xa[:, SSM_INNER + SSM_GROUPS * SSM_STATE:]
    dt = jax.nn.softplus(misc_ref[:, LANE:2 * LANE] + dtb_ref[...])
    da = dt * (-jnp.exp(alog_ref[...]))
    ri = lax.broadcasted_iota(jnp.int32, (L, L), 0)
    ci = lax.broadcasted_iota(jnp.int32, (L, L), 1)
    causal = ri >= ci
    acs = _dot(causal.astype(F32), da, HI)
    acs_t = acs.T
    eacs = jnp.exp(acs)
    last = acs[L - 1:L, :]
    dte = jnp.exp(last - acs)
    elast = jnp.exp(last)
    dpar = dpar_ref[...]
    hg = SSM_HEADS // SSM_GROUPS
    for g in range(SSM_GROUPS):
        bg = bm[:, g * SSM_STATE:(g + 1) * SSM_STATE]
        cg = cm[:, g * SSM_STATE:(g + 1) * SSM_STATE]
        cbm = _dot_nt(cg, bg, HI)
        for hh in range(hg):
            h = g * hg + hh
            seg = jnp.where(causal, jnp.exp(acs[:, h:h + 1] - acs_t[h:h + 1, :]), 0.0)
            xh = xa[:, h * SSM_HEAD_DIM:(h + 1) * SSM_HEAD_DIM]
            xdt = xh * dt[:, h:h + 1]
            s_old = s_ref[h]
            yh = (_dot(cbm * seg, xdt, HI) + eacs[:, h:h + 1] * _dot_nt(cg, s_old, HI)
                  + dpar[:, h:h + 1] * xh)
            ysc_ref[:, h * SSM_HEAD_DIM:(h + 1) * SSM_HEAD_DIM] = yh
            s_ref[h] = elast[:, h:h + 1] * s_old + _dot_tn(xdt * dte[:, h:h + 1], bg, HI)
    yv = ysc_ref[...] * _silu(z_ref[...])
    gw = SSM_INNER // SSM_GROUPS
    nw = nw_ref[...]
    for g in range(SSM_GROUPS):
        yg = yv[:, g * gw:(g + 1) * gw]
        y_ref[:, g * gw:(g + 1) * gw] = _rms(yg, nw[:, g * gw:(g + 1) * gw])

    @pl.when(c == nc - 1)
    def _():
        sfin_ref[...] = s_ref[...]


def _mamba(p, cbuf, s0, cw, cbias, dtb, alog, dpar, nw, row0, nseq, t):
    L = min(SSM_CHUNK, t)
    nc = t // L
    blk0 = row0 // L
    kern = functools.partial(_mamba_kernel, L=L, nc=nc)
    vec = lambda w: pl.BlockSpec((1, w), lambda b, c: (0, 0))
    return pl.pallas_call(
        kern,
        out_shape=[jax.ShapeDtypeStruct((nseq * t, SSM_INNER), F32),
                   jax.ShapeDtypeStruct((nseq, SSM_HEADS, SSM_HEAD_DIM, SSM_STATE), F32)],
        grid=(nseq, nc),
        in_specs=[pl.BlockSpec((L, SSM_INNER), lambda b, c: (blk0 + b * nc + c, COL_Z // SSM_INNER)),
                  pl.BlockSpec((L, SSM_CONV_DIM), lambda b, c: (blk0 + b * nc + c, COL_XBC // SSM_CONV_DIM)),
                  pl.BlockSpec((L, MISC_W), lambda b, c: (blk0 + b * nc + c, COL_MISC // MISC_W)),
                  pl.BlockSpec((None, SSM_CONV - 1, SSM_CONV_DIM), lambda b, c: (b, 0, 0)),
                  pl.BlockSpec((None, SSM_HEADS, SSM_HEAD_DIM, SSM_STATE), lambda b, c: (b, 0, 0, 0)),
                  pl.BlockSpec((SSM_CONV, SSM_CONV_DIM), lambda b, c: (0, 0)),
                  vec(SSM_CONV_DIM), vec(LANE), vec(LANE), vec(LANE), vec(SSM_INNER)],
        out_specs=[pl.BlockSpec((L, SSM_INNER), lambda b, c: (b * nc + c, 0)),
                   pl.BlockSpec((None, SSM_HEADS, SSM_HEAD_DIM, SSM_STATE), lambda b, c: (b, 0, 0, 0))],
        scratch_shapes=[pltpu.VMEM((L + 8, SSM_CONV_DIM), F32),
                        pltpu.VMEM((SSM_HEADS, SSM_HEAD_DIM, SSM_STATE), F32),
                        pltpu.VMEM((L, SSM_INNER), F32)],
        compiler_params=_params("arbitrary", "arbitrary"),
        name="ssd_scan",
    )(p, p, p, cbuf, s0, cw, cbias, dtb, alog, dpar, nw)


def _gla_kernel(gq_ref, gk_ref, gv_ref, og_ref, misc_ref, s0_ref, wg_ref, bg_ref, gn_ref, o_ref, sfin_ref,
                st_ref, gl_ref, *, L, ls, nc):
    c = pl.program_id(1)

    @pl.when(c == 0)
    def _():
        for h in range(GLA_HEADS):
            st_ref[h] = s0_ref[h].T

    gl_ref[...] = jax.nn.log_sigmoid(_dot(misc_ref[:, LANE:2 * LANE], wg_ref[...], HI) + bg_ref[...]) \
        * (1.0 / GLA_GATE_NORMALIZER)
    ri = lax.broadcasted_iota(jnp.int32, (ls, ls), 0)
    ci = lax.broadcasted_iota(jnp.int32, (ls, ls), 1)
    tril = (ri >= ci).astype(F32)
    rows = lax.broadcasted_iota(jnp.int32, (ls, GLA_DK), 0)
    gn = gn_ref[...]

    def sub_chunk(sidx, carry):
        r0 = pl.multiple_of(sidx * ls, ls)
        for h in range(GLA_HEADS):
            q = gq_ref[pl.ds(r0, ls), h * GLA_DK:(h + 1) * GLA_DK] * (GLA_DK ** -0.5)
            k = gk_ref[pl.ds(r0, ls), h * GLA_DK:(h + 1) * GLA_DK]
            v = gv_ref[pl.ds(r0, ls), h * GLA_DV:(h + 1) * GLA_DV]
            g = gl_ref[pl.ds(r0, ls), h * GLA_DK:(h + 1) * GLA_DK]
            bcs = _dot(tril, g, HI)
            st = st_ref[h]
            o = _dot_nt(q * jnp.exp(bcs), st, HI)
            for j in range(ls):
                dj = jnp.where(rows >= j, bcs - bcs[j:j + 1, :], NEG_INF)
                w = jnp.sum(q * k[j:j + 1, :] * jnp.exp(dj), axis=-1, keepdims=True)
                o = o + w * v[j:j + 1, :]
            lastb = bcs[ls - 1:ls, :]
            st_ref[h] = st * jnp.exp(lastb) + _dot_tn(v, k * jnp.exp(lastb - bcs), HI)
            o = o * lax.rsqrt(jnp.mean(o * o, axis=-1, keepdims=True) + EPS) * gn
            o = o * _silu(og_ref[pl.ds(r0, ls), h * GLA_DV:(h + 1) * GLA_DV])
            o_ref[pl.ds(r0, ls), h * GLA_DV:(h + 1) * GLA_DV] = o
        return carry

    lax.fori_loop(0, L // ls, sub_chunk, 0)

    @pl.when(c == nc - 1)
    def _():
        for h in range(GLA_HEADS):
            sfin_ref[h] = st_ref[h].T


def _gla(p, s0, wg, bg, gn, row0, nseq, t):
    L = min(128, t)
    ls = min(GLA_SUB, L)
    nc = t // L
    blk0 = row0 // L
    kern = functools.partial(_gla_kernel, L=L, ls=ls, nc=nc)
    rowblk = lambda w, col: pl.BlockSpec((L, w), lambda b, c: (blk0 + b * nc + c, col // w))
    return pl.pallas_call(
        kern,
        out_shape=[jax.ShapeDtypeStruct((nseq * t, GLA_VALUE), F32),
                   jax.ShapeDtypeStruct((nseq, GLA_HEADS, GLA_DK, GLA_DV), F32)],
        grid=(nseq, nc),
        in_specs=[rowblk(GLA_KEY, COL_GQ), rowblk(GLA_KEY, COL_GK), rowblk(GLA_VALUE, COL_GV),
                  rowblk(GLA_VALUE, COL_OG), rowblk(MISC_W, COL_MISC),
                  pl.BlockSpec((None, GLA_HEADS, GLA_DK, GLA_DV), lambda b, c: (b, 0, 0, 0)),
                  pl.BlockSpec((LANE, GLA_KEY), lambda b, c: (0, 0)),
                  pl.BlockSpec((1, GLA_KEY), lambda b, c: (0, 0)),
                  pl.BlockSpec((1, GLA_DV), lambda b, c: (0, 0))],
        out_specs=[pl.BlockSpec((L, GLA_VALUE), lambda b, c: (b * nc + c, 0)),
                   pl.BlockSpec((None, GLA_HEADS, GLA_DK, GLA_DV), lambda b, c: (b, 0, 0, 0))],
        scratch_shapes=[pltpu.VMEM((GLA_HEADS, GLA_DV, GLA_DK), F32),
                        pltpu.VMEM((L, GLA_KEY), F32)],
        compiler_params=_params("arbitrary", "arbitrary"),
        name="gla_scan",
    )(p, p, p, p, p, s0, wg, bg, gn)


def _merge_kernel(oa_ref, ob_ref, oc_ref, ga_ref, gb_ref, gc_ref, wb_ref, o_ref):
    acc = jax.nn.sigmoid(ga_ref[...]) * _dot(oa_ref[...].astype(BF16), wb_ref[0])
    acc += jax.nn.sigmoid(gb_ref[...]) * _dot(ob_ref[...].astype(BF16), wb_ref[1])
    acc += jax.nn.sigmoid(gc_ref[...]) * _dot(oc_ref[...].astype(BF16), wb_ref[2])
    o_ref[...] = acc.astype(BF16)


def _merge(oa, ob, oc, p, wb, tm_pref=512, tn=512):
    rows = oa.shape[0]
    tm = _tile(rows, tm_pref, 16)
    nj = D_MODEL // tn
    br = lambda: pl.BlockSpec((tm, BRANCH_W), lambda j, i: (i, 0))
    gate = lambda n: pl.BlockSpec((tm, tn), lambda j, i: (i, (COL_GATES + n * D_MODEL) // tn + j))
    return pl.pallas_call(
        _merge_kernel,
        out_shape=jax.ShapeDtypeStruct((rows, D_MODEL), BF16),
        grid=(nj, rows // tm),
        in_specs=[br(), br(), br(), gate(0), gate(1), gate(2),
                  pl.BlockSpec((3, BRANCH_W, tn), lambda j, i: (0, 0, j))],
        out_specs=pl.BlockSpec((tm, tn), lambda j, i: (i, j)),
        compiler_params=_params("arbitrary", "arbitrary"),
        name="branch_merge",
    )(oa, ob, oc, p, p, p, wb)


def _out_proj_kernel(m_ref, w_ref, x_ref, o_ref):
    o_ref[...] = x_ref[...] + _dot(m_ref[...], w_ref[...])


def _out_proj(m, w, x, tm_pref=512, tn=512):
    rows, k = m.shape
    n = w.shape[1]
    tm = _tile(rows, tm_pref, 16)
    return pl.pallas_call(
        _out_proj_kernel,
        out_shape=jax.ShapeDtypeStruct((rows, n), F32),
        grid=(n // tn, rows // tm),
        in_specs=[pl.BlockSpec((tm, k), lambda j, i: (i, 0)),
                  pl.BlockSpec((k, tn), lambda j, i: (0, j)),
                  pl.BlockSpec((tm, tn), lambda j, i: (i, j))],
        out_specs=pl.BlockSpec((tm, tn), lambda j, i: (i, j)),
        compiler_params=_params("arbitrary", "arbitrary"),
        name="out_proj",
    )(m, w, x)


def _top16(s):
    n = s.shape[0]
    iota = lax.broadcasted_iota(jnp.int32, s.shape, 0)
    rank = jnp.full(s.shape, PEER_TOPK, jnp.int32)
    vals = []
    for r in range(PEER_TOPK):
        m = jnp.max(s, axis=0, keepdims=True)
        pos = jnp.min(jnp.where(s == m, iota, n), axis=0, keepdims=True)
        hit = iota == pos
        rank = jnp.where(hit, r, rank)
        s = jnp.where(hit, NEG_INF, s)
        vals.append(m)
    return rank, vals


def _peer_route_kernel(x_ref, g_ref, wqt_ref, sk_ref, hq_ref, rank2_ref, cnt_ref, a_ref, b_ref):
    hq = _rms(x_ref[...], g_ref[...]).astype(BF16)
    hq_ref[...] = hq
    qt = _dot_nt(wqt_ref[...], hq)
    t = qt.shape[1]
    iota_c = lax.broadcasted_iota(jnp.int32, (PEER_TOPK * PEER_TOPK, t), 0)
    iota_r = lax.broadcasted_iota(jnp.int32, (PEER_TOPK, t), 0)
    for h in range(PEER_HEADS):
        base = h * 2 * PEER_HALF
        s1 = _dot(sk_ref[0, h], qt[base:base + PEER_HALF].astype(BF16))
        s2 = _dot(sk_ref[1, h], qt[base + PEER_HALF:base + 2 * PEER_HALF].astype(BF16))
        rank1, v1 = _top16(s1)
        rank2, v2 = _top16(s2)
        v2s = jnp.concatenate(v2, axis=0)
        cand = jnp.concatenate([v1[r] + v2s for r in range(PEER_TOPK)], axis=0)
        top = v1[0] + v2[0]
        cnt = jnp.zeros((PEER_TOPK, t), F32)
        z = jnp.zeros((1, t), F32)
        for _ in range(PEER_TOPK):
            m = jnp.max(cand, axis=0, keepdims=True)
            pos = jnp.min(jnp.where(cand == m, iota_c, PEER_TOPK * PEER_TOPK), axis=0, keepdims=True)
            z = z + jnp.exp(m - top)
            cnt = cnt + (iota_r == jnp.right_shift(pos, 4)).astype(F32)
            cand = jnp.where(iota_c == pos, NEG_INF, cand)
        cntrow = jnp.zeros(s1.shape, F32)
        for r in range(PEER_TOPK):
            cntrow = jnp.where(rank1 == r, cnt[r:r + 1, :], cntrow)
        rank2_ref[h] = rank2.astype(F32)
        cnt_ref[h] = cntrow
        a_ref[h] = jnp.where(rank1 < PEER_TOPK, jnp.exp(s1 - v1[0]), 0.0) / z
        b_ref[h] = jnp.where(rank2 < PEER_TOPK, jnp.exp(s2 - v2[0]), 0.0)


def _peer_route(x, g, wqt, sk, tt):
    rows, d = x.shape
    tab = jax.ShapeDtypeStruct((PEER_HEADS, PEER_NKEYS, rows), F32)
    tab_spec = pl.BlockSpec((PEER_HEADS, PEER_NKEYS, tt), lambda i: (0, 0, i))
    return pl.pallas_call(
        _peer_route_kernel,
        out_shape=[jax.ShapeDtypeStruct((rows, d), BF16), tab, tab, tab, tab],
        grid=(rows // tt,),
        in_specs=[pl.BlockSpec((tt, d), lambda i: (i, 0)),
                  pl.BlockSpec((1, d), lambda i: (0, 0)),
                  pl.BlockSpec(wqt.shape, lambda i: (0, 0)),
                  pl.BlockSpec(sk.shape, lambda i: (0, 0, 0, 0))],
        out_specs=[pl.BlockSpec((tt, d), lambda i: (i, 0)), tab_spec, tab_spec, tab_spec, tab_spec],
        compiler_params=_params("arbitrary"),
        name="peer_route",
    )(x, g, wqt, sk)


def _peer_dense_kernel(hq_ref, x_ref, u_ref, vt_ref, rank2_ref, cnt_ref, a_ref, b_ref, o_ref, acc_ref, wa_ref,
                       *, eb, ne):
    e = pl.program_id(1)

    @pl.when(e == 0)
    def _():
        acc_ref[...] = jnp.zeros(acc_ref.shape, F32)

    act = _dot_nt(u_ref[...], hq_ref[...])
    act = 0.5 * act * (1.0 + lax.erf(act * (2.0 ** -0.5)))
    nsub = eb // PEER_NKEYS
    for ii in range(nsub):
        i = e * nsub + ii
        w = None
        for h in range(PEER_HEADS):
            c = cnt_ref[h, pl.ds(i, 1), :]
            av = a_ref[h, pl.ds(i, 1), :]
            term = av * jnp.where(rank2_ref[h] < c, b_ref[h], 0.0)
            w = term if w is None else w + term
        wa_ref[ii * PEER_NKEYS:(ii + 1) * PEER_NKEYS, :] = (w * act[ii * PEER_NKEYS:(ii + 1) * PEER_NKEYS]).astype(BF16)
    acc_ref[...] += _dot(vt_ref[...], wa_ref[...])

    @pl.when(e == ne - 1)
    def _():
        o_ref[...] = x_ref[...] + acc_ref[...].T


def _peer_dense(hq, x, u, vt, rank2, cntrow, a, b, tt, eb=512):
    rows, d = x.shape
    n_exp = u.shape[0]
    ne = n_exp // eb
    kern = functools.partial(_peer_dense_kernel, eb=eb, ne=ne)
    tab_spec = pl.BlockSpec((PEER_HEADS, PEER_NKEYS, tt), lambda i, e: (0, 0, i))
    return pl.pallas_call(
        kern,
        out_shape=jax.ShapeDtypeStruct((rows, d), F32),
        grid=(rows // tt, ne),
        in_specs=[pl.BlockSpec((tt, d), lambda i, e: (i, 0)),
                  pl.BlockSpec((tt, d), lambda i, e: (i, 0)),
                  pl.BlockSpec((eb, d), lambda i, e: (e, 0)),
                  pl.BlockSpec((d, eb), lambda i, e: (0, e)),
                  tab_spec, tab_spec, tab_spec, tab_spec],
        out_specs=pl.BlockSpec((tt, d), lambda i, e: (i, 0)),
        scratch_shapes=[pltpu.VMEM((d, tt), F32), pltpu.VMEM((eb, tt), BF16)],
        compiler_params=_params("arbitrary", "arbitrary"),
        name="peer_dense",
    )(hq, x, u, vt, rank2, cntrow, a, b)


def _final_norm_kernel(x_ref, g_ref, o_ref):
    o_ref[...] = _rms(x_ref[...], g_ref[...])


def _final_norm(x, g, tm_pref=512):
    rows, d = x.shape
    tm = _tile(rows, tm_pref)
    return pl.pallas_call(
        _final_norm_kernel,
        out_shape=jax.ShapeDtypeStruct((rows, d), F32),
        grid=(rows // tm,),
        in_specs=[pl.BlockSpec((tm, d), lambda i: (i, 0)), pl.BlockSpec((1, d), lambda i: (0, 0))],
        out_specs=pl.BlockSpec((tm, d), lambda i: (i, 0)),
        compiler_params=_params("arbitrary"),
        name="final_norm",
    )(x, g)


def _pad_lanes(v, width=LANE):
    return jnp.pad(v, [(0, 0)] * (v.ndim - 1) + [(0, width - v.shape[-1])])


def _rot_half(w):
    half = w.shape[-1] // 2
    return jnp.concatenate([w[..., half:], w[..., :half]], axis=-1)


def _layout_w_in(w_in):
    widths = (MLA_Q_RANK, MLA_KV_RANK, MLA_ROPE, SSM_INNER, SSM_CONV_DIM, SSM_HEADS,
              GLA_KEY, GLA_KEY, GLA_VALUE, GLA_GATE_RANK, GLA_VALUE, 3 * D_MODEL)
    offs = [0]
    for w in widths:
        offs.append(offs[-1] + w)
    cq, ckv, kr, z, xbc, dt, gq, gk, gv, glr, og, gates = [w_in[..., offs[i]:offs[i + 1]] for i in range(len(widths))]
    zero = jnp.zeros(w_in.shape[:-1] + (MISC_W - 2 * MLA_ROPE - SSM_HEADS - GLA_GATE_RANK,), w_in.dtype)
    out = jnp.concatenate([gates, z, og, gv, xbc, gq, gk, cq, ckv, kr, _rot_half(kr), dt, glr, zero], axis=-1)
    return out.astype(BF16)


def _rope_tables(pos):
    half = MLA_ROPE // 2
    inv = ROPE_BASE ** (-jnp.arange(half, dtype=F32) / half)
    ang = pos.astype(F32)[:, None] * inv[None, :]
    cos, sin = jnp.cos(ang), jnp.sin(ang)
    cos_t = jnp.tile(jnp.concatenate([cos, cos], axis=-1), (1, MLA_HEADS))
    sin_t = jnp.tile(jnp.concatenate([-sin, sin], axis=-1), (1, MLA_HEADS))
    return cos_t, sin_t


def kernel(x_prompt, x_sample, cache_mla_latent, cache_mla_krope, page_table, state_ssm_conv, state_ssm, state_gla, norm_mix, w_in, mla_q_norm, mla_w_uq, mla_kv_norm, mla_w_uk, mla_w_uv, ssm_conv_w, ssm_conv_b, ssm_dt_bias, ssm_a_log, ssm_d, ssm_norm, gla_gate_w, gla_gate_b, gla_norm, w_branch, w_out, norm_ffn, peer_wq, peer_subkeys, peer_u, peer_v, norm_final):
    bp, tp, d = x_prompt.shape
    bs, ts, _ = x_sample.shape
    depth = w_in.shape[0]
    n_pages = page_table.shape[1]
    past_len = n_pages * cache_mla_latent.shape[2]
    rows_p = bp * tp
    rows_s = bs * ts
    rows = rows_p + rows_s

    x = jnp.concatenate([x_prompt.reshape(rows_p, d), x_sample.reshape(rows_s, d)], axis=0)
    pos = jnp.concatenate([jnp.tile(jnp.arange(tp), bp), jnp.tile(past_len + jnp.arange(ts), bs)])
    cos_t, sin_t = _rope_tables(pos)
    w_in_l = _layout_w_in(w_in)
    uq_nope = mla_w_uq[..., :MLA_NOPE].reshape(depth, MLA_Q_RANK, MLA_HEADS * MLA_NOPE)
    uq_rope = mla_w_uq[..., MLA_NOPE:]
    wq_all = jnp.concatenate([uq_nope,
                              uq_rope.reshape(depth, MLA_Q_RANK, MLA_HEADS * MLA_ROPE),
                              _rot_half(uq_rope).reshape(depth, MLA_Q_RANK, MLA_HEADS * MLA_ROPE)],
                             axis=-1).astype(BF16)
    wuk_t = jnp.transpose(mla_w_uk, (0, 2, 3, 1)).astype(BF16)
    wuv = jnp.transpose(mla_w_uv, (0, 2, 1, 3)).astype(BF16)
    dtb = _pad_lanes(ssm_dt_bias)
    alog = _pad_lanes(ssm_a_log)
    dpar = _pad_lanes(ssm_d)
    wg = jnp.zeros((depth, LANE, GLA_KEY), F32).at[:, SSM_HEADS:SSM_HEADS + GLA_GATE_RANK, :].set(gla_gate_w)
    wb = w_branch.astype(BF16)
    wo = w_out.astype(BF16)
    wqt = jnp.swapaxes(peer_wq, 1, 2).astype(BF16)
    sk = peer_subkeys.astype(BF16)
    ub = peer_u.astype(BF16)
    vtb = jnp.swapaxes(peer_v, 1, 2).astype(BF16)
    conv0 = jnp.zeros((bp, SSM_CONV - 1, SSM_CONV_DIM), F32)
    ssm0 = jnp.zeros((bp, SSM_HEADS, SSM_HEAD_DIM, SSM_STATE), F32)
    gla0 = jnp.zeros((bp, GLA_HEADS, GLA_DK, GLA_DV), F32)
    tt = _tile(rows, 256, LANE)

    new_p = [[], [], [], [], []]
    new_s = [[], [], [], [], []]
    for l in range(depth):
        p = _norm_matmul(x, norm_mix[l][None], w_in_l[l])
        ql, qr, lat, kr = _mla_prep(p, cos_t, sin_t, mla_q_norm[l][None], mla_kv_norm[l][None], wq_all[l], wuk_t[l])
        oa_p = _attn_prompt(ql, qr, lat, kr, wuv[l], bp, tp)
        oa_s = _attn_sample(ql, qr, lat, kr, cache_mla_latent, cache_mla_krope, page_table, l, wuv[l],
                            rows_p, bs, ts)
        ssm_args = (ssm_conv_w[l], ssm_conv_b[l][None], dtb[l][None], alog[l][None], dpar[l][None],
                    ssm_norm[l][None])
        ob_p, ssm_p = _mamba(p, conv0, ssm0, *ssm_args, 0, bp, tp)
        ob_s, ssm_s = _mamba(p, state_ssm_conv[l], state_ssm[l], *ssm_args, rows_p, bs, ts)
        gla_args = (wg[l], gla_gate_b[l][None], gla_norm[l][None])
        oc_p, gla_p = _gla(p, gla0, *gla_args, 0, bp, tp)
        oc_s, gla_s = _gla(p, state_gla[l], *gla_args, rows_p, bs, ts)
        merged = _merge(jnp.concatenate([oa_p, oa_s]), jnp.concatenate([ob_p, ob_s]),
                        jnp.concatenate([oc_p, oc_s]), p, wb[l])
        x = _out_proj(merged, wo[l], x)
        hq, rank2, cntrow, ga, gb = _peer_route(x, norm_ffn[l][None], wqt[l], sk[l], tt)
        x = _peer_dense(hq, x, ub[l], vtb[l], rank2, cntrow, ga, gb, tt)

        xbc = p[:, COL_XBC:COL_XBC + SSM_CONV_DIM]
        conv_p = xbc[:rows_p].reshape(bp, tp, SSM_CONV_DIM)[:, tp - (SSM_CONV - 1):]
        conv_s = xbc[rows_p:].reshape(bs, ts, SSM_CONV_DIM)[:, ts - (SSM_CONV - 1):]
        for lst, a in zip(new_p, (lat[:rows_p].reshape(bp, tp, MLA_KV_RANK), kr[:rows_p].reshape(bp, tp, MLA_ROPE),
                                  conv_p, ssm_p, gla_p)):
            lst.append(a)
        for lst, a in zip(new_s, (lat[rows_p:].reshape(bs, ts, MLA_KV_RANK), kr[rows_p:].reshape(bs, ts, MLA_ROPE),
                                  conv_s, ssm_s, gla_s)):
            lst.append(a)

    y = _final_norm(x, norm_final[None])
    y_prompt = y[:rows_p].reshape(bp, tp, d)
    y_sample = y[rows_p:].reshape(bs, ts, d)
    outs_p = [jnp.stack(a) for a in new_p]
    outs_s = [jnp.stack(a) for a in new_s]
    return (y_prompt, y_sample, *outs_p, *outs_s)
```

```python
import functools
import math

import jax
import jax.numpy as jnp
from jax import lax
from jax.experimental import pallas as pl
from jax.experimental.pallas import tpu as pltpu

F32 = jnp.float32
BF16 = jnp.bfloat16
HI = lax.Precision.HIGHEST
NEG_INF = float("-inf")

EPS = 1e-6
D_MODEL = 2048
BRANCH_W = 1024
MLA_HEADS = 8
MLA_NOPE = 128
MLA_ROPE = 64
MLA_V = 128
MLA_Q_RANK = 512
MLA_KV_RANK = 256
MLA_SCALE = (MLA_NOPE + MLA_ROPE) ** -0.5
ROPE_BASE = 10000.0
SSM_INNER = 1024
SSM_HEAD_DIM = 64
SSM_HEADS = 16
SSM_GROUPS = 2
SSM_STATE = 128
SSM_CONV = 4
SSM_CONV_DIM = SSM_INNER + 2 * SSM_GROUPS * SSM_STATE
SSM_CHUNK = 128
GLA_HEADS = 4
GLA_DK = 128
GLA_DV = 256
GLA_KEY = 512
GLA_VALUE = 1024
GLA_GATE_RANK = 16
GLA_GATE_NORMALIZER = 16.0
GLA_SUB = 16
PEER_HEADS = 8
PEER_NKEYS = 128
PEER_TOPK = 16
PEER_HALF = 128

LANE = 128
VMEM_LIMIT = 56 * 1024 * 1024

COL_GATES = 0
COL_Z = 6144
COL_OG = 7168
COL_GV = 8192
COL_XBC = 9216
COL_GQ = 10752
COL_GK = 11264
COL_CQ = 11776
COL_CKV = 12288
COL_MISC = 12544
IN_COLS = 12800
MISC_W = 256


def _tile(n, pref, mult=8):
    for t in range(min(n, pref), 0, -1):
        if n % t == 0 and t % mult == 0:
            return t
    return n


def _params(*sem):
    return pltpu.CompilerParams(dimension_semantics=sem, vmem_limit_bytes=VMEM_LIMIT)


def _rms(x, g):
    return x * lax.rsqrt(jnp.mean(x * x, axis=-1, keepdims=True) + EPS) * g


def _dot(a, b, prec=None):
    return jnp.dot(a, b, preferred_element_type=F32, precision=prec)


def _dot_nt(a, b, prec=None):
    return lax.dot_general(a, b, (((1,), (1,)), ((), ())), preferred_element_type=F32, precision=prec)


def _dot_tn(a, b, prec=None):
    return lax.dot_general(a, b, (((0,), (0,)), ((), ())), preferred_element_type=F32, precision=prec)


def _silu(x):
    return x * jax.nn.sigmoid(x)


def _norm_matmul_kernel(x_ref, g_ref, w_ref, o_ref, h_ref):
    @pl.when(pl.program_id(1) == 0)
    def _():
        h_ref[...] = _rms(x_ref[...], g_ref[...]).astype(BF16)

    o_ref[...] = _dot(h_ref[...], w_ref[...])


def _norm_matmul(x, g, w, tm_pref=1024, tn=512):
    m, k = x.shape
    n = w.shape[1]
    tm = _tile(m, tm_pref)
    return pl.pallas_call(
        _norm_matmul_kernel,
        out_shape=jax.ShapeDtypeStruct((m, n), F32),
        grid=(m // tm, n // tn),
        in_specs=[pl.BlockSpec((tm, k), lambda i, j: (i, 0)),
                  pl.BlockSpec((1, k), lambda i, j: (0, 0)),
                  pl.BlockSpec((k, tn), lambda i, j: (0, j))],
        out_specs=pl.BlockSpec((tm, tn), lambda i, j: (i, j)),
        scratch_shapes=[pltpu.VMEM((tm, k), BF16)],
        compiler_params=_params("arbitrary", "arbitrary"),
        name="norm_in_proj",
    )(x, g, w)


def _mla_prep_kernel(cq_ref, ckv_ref, misc_ref, cos_ref, sin_ref, qn_ref, kvn_ref, wq_ref, wuk_ref,
                     ql_ref, qr_ref, lat_ref, kr_ref):
    cqn = _rms(cq_ref[...], qn_ref[...]).astype(BF16)
    q = _dot(cqn, wq_ref[...])
    cos = cos_ref[...]
    sin = sin_ref[...]
    nope_w = MLA_HEADS * MLA_NOPE
    rope_w = MLA_HEADS * MLA_ROPE
    qr_ref[...] = (q[:, nope_w:nope_w + rope_w] * cos + q[:, nope_w + rope_w:] * sin) * MLA_SCALE
    for h in range(MLA_HEADS):
        qh = q[:, h * MLA_NOPE:(h + 1) * MLA_NOPE].astype(BF16)
        ql_ref[:, h * MLA_KV_RANK:(h + 1) * MLA_KV_RANK] = _dot(qh, wuk_ref[h]) * MLA_SCALE
    lat_ref[...] = _rms(ckv_ref[...], kvn_ref[...])
    misc = misc_ref[...]
    kr_ref[...] = misc[:, 0:MLA_ROPE] * cos[:, 0:MLA_ROPE] + misc[:, MLA_ROPE:2 * MLA_ROPE] * sin[:, 0:MLA_ROPE]


def _mla_prep(p, cos, sin, qn, kvn, wq, wuk, tm_pref=256):
    rows = p.shape[0]
    tm = _tile(rows, tm_pref)
    lat_w = MLA_HEADS * MLA_KV_RANK
    rope_w = MLA_HEADS * MLA_ROPE
    return pl.pallas_call(
        _mla_prep_kernel,
        out_shape=[jax.ShapeDtypeStruct((rows, lat_w), F32),
                   jax.ShapeDtypeStruct((rows, rope_w), F32),
                   jax.ShapeDtypeStruct((rows, MLA_KV_RANK), F32),
                   jax.ShapeDtypeStruct((rows, MLA_ROPE), F32)],
        grid=(rows // tm,),
        in_specs=[pl.BlockSpec((tm, MLA_Q_RANK), lambda i: (i, COL_CQ // MLA_Q_RANK)),
                  pl.BlockSpec((tm, MLA_KV_RANK), lambda i: (i, COL_CKV // MLA_KV_RANK)),
                  pl.BlockSpec((tm, MISC_W), lambda i: (i, COL_MISC // MISC_W)),
                  pl.BlockSpec((tm, rope_w), lambda i: (i, 0)),
                  pl.BlockSpec((tm, rope_w), lambda i: (i, 0)),
                  pl.BlockSpec((1, MLA_Q_RANK), lambda i: (0, 0)),
                  pl.BlockSpec((1, MLA_KV_RANK), lambda i: (0, 0)),
                  pl.BlockSpec(wq.shape, lambda i: (0, 0)),
                  pl.BlockSpec(wuk.shape, lambda i: (0, 0, 0))],
        out_specs=[pl.BlockSpec((tm, lat_w), lambda i: (i, 0)),
                   pl.BlockSpec((tm, rope_w), lambda i: (i, 0)),
                   pl.BlockSpec((tm, MLA_KV_RANK), lambda i: (i, 0)),
                   pl.BlockSpec((tm, MLA_ROPE), lambda i: (i, 0))],
        compiler_params=_params("arbitrary"),
        name="mla_prep",
    )(p, p, p, cos, sin, qn, kvn, wq, wuk)


def _attn_prompt_kernel(ql_ref, qr_ref, k_ref, kr_ref, wuvt_ref, o_ref, qs, qrs, m_ref, l_ref, acc_ref, *, tq, tk):
    qi = pl.program_id(1)
    ki = pl.program_id(2)
    ncol = MLA_HEADS * tq

    @pl.when(ki == 0)
    def _():
        for h in range(MLA_HEADS):
            qs[h * tq:(h + 1) * tq, :] = ql_ref[:, h * MLA_KV_RANK:(h + 1) * MLA_KV_RANK].astype(BF16)
            qrs[h * tq:(h + 1) * tq, :] = qr_ref[:, h * MLA_ROPE:(h + 1) * MLA_ROPE].astype(BF16)
        m_ref[...] = jnp.full(m_ref.shape, NEG_INF, F32)
        l_ref[...] = jnp.zeros(l_ref.shape, F32)
        acc_ref[...] = jnp.zeros(acc_ref.shape, F32)

    @pl.when(ki <= qi)
    def _():
        k = k_ref[...].astype(BF16)
        kr = kr_ref[...].astype(BF16)
        st = _dot_nt(k, qs[...]) + _dot_nt(kr, qrs[...])
        kpos = ki * tk + lax.broadcasted_iota(jnp.int32, (tk, ncol), 0)
        qpos = qi * tq + (lax.broadcasted_iota(jnp.int32, (tk, ncol), 1) & (tq - 1))
        st = jnp.where(kpos <= qpos, st, NEG_INF)
        m_prev = m_ref[...]
        m_new = jnp.maximum(m_prev, jnp.max(st, axis=0, keepdims=True))
        p = jnp.exp(st - m_new)
        alpha = jnp.exp(m_prev - m_new)
        l_ref[...] = alpha * l_ref[...] + jnp.sum(p, axis=0, keepdims=True)
        acc_ref[...] = alpha * acc_ref[...] + _dot_tn(k, p.astype(BF16))
        m_ref[...] = m_new

    @pl.when(ki == qi)
    def _():
        o = (acc_ref[...] / l_ref[...]).astype(BF16)
        for h in range(MLA_HEADS):
            o_ref[:, h * MLA_V:(h + 1) * MLA_V] = _dot(wuvt_ref[h], o[:, h * tq:(h + 1) * tq]).T


def _attn_prompt(ql, qr, lat, kr, wuvt, nb, t, tq_pref=256):
    tq = _tile(t, tq_pref)
    assert tq & (tq - 1) == 0
    nq = t // tq
    ncol = MLA_HEADS * tq
    kern = functools.partial(_attn_prompt_kernel, tq=tq, tk=tq)
    return pl.pallas_call(
        kern,
        out_shape=jax.ShapeDtypeStruct((nb * t, MLA_HEADS * MLA_V), F32),
        grid=(nb, nq, nq),
        in_specs=[pl.BlockSpec((tq, MLA_HEADS * MLA_KV_RANK), lambda b, qi, ki: (b * nq + qi, 0)),
                  pl.BlockSpec((tq, MLA_HEADS * MLA_ROPE), lambda b, qi, ki: (b * nq + qi, 0)),
                  pl.BlockSpec((tq, MLA_KV_RANK), lambda b, qi, ki: (b * nq + jnp.minimum(ki, qi), 0)),
                  pl.BlockSpec((tq, MLA_ROPE), lambda b, qi, ki: (b * nq + jnp.minimum(ki, qi), 0)),
                  pl.BlockSpec(wuvt.shape, lambda b, qi, ki: (0, 0, 0))],
        out_specs=pl.BlockSpec((tq, MLA_HEADS * MLA_V), lambda b, qi, ki: (b * nq + qi, 0)),
        scratch_shapes=[pltpu.VMEM((ncol, MLA_KV_RANK), BF16),
                        pltpu.VMEM((ncol, MLA_ROPE), BF16),
                        pltpu.VMEM((1, ncol), F32),
                        pltpu.VMEM((1, ncol), F32),
                        pltpu.VMEM((MLA_KV_RANK, ncol), F32)],
        compiler_params=_params("arbitrary", "arbitrary", "arbitrary"),
        name="attn_prompt",
    )(ql, qr, lat, kr, wuvt)


def _attn_sample_kernel(pt_ref, ql_ref, qr_ref, kn_ref, krn_ref, wuv_ref, *rest, pages, npg, ts, page):
    lat_refs = rest[:pages]
    kr_refs = rest[pages:2 * pages]
    o_ref = rest[2 * pages]
    qs, qrs, kcat, krcat, knp, krnp, m_ref, l_ref, acc_ref = rest[2 * pages + 1:]
    j = pl.program_id(1)
    nrow = MLA_HEADS * ts

    @pl.when(j == 0)
    def _():
        for h in range(MLA_HEADS):
            qs[h * ts:(h + 1) * ts, :] = ql_ref[:, h * MLA_KV_RANK:(h + 1) * MLA_KV_RANK]
            qrs[h * ts:(h + 1) * ts, :] = qr_ref[:, h * MLA_ROPE:(h + 1) * MLA_ROPE]
        knp[...] = jnp.zeros(knp.shape, F32)
        krnp[...] = jnp.zeros(krnp.shape, F32)
        knp[0:ts, :] = kn_ref[...]
        krnp[0:ts, :] = krn_ref[...]
        m_ref[...] = jnp.full(m_ref.shape, NEG_INF, F32)
        l_ref[...] = jnp.zeros(l_ref.shape, F32)
        acc_ref[...] = jnp.zeros(acc_ref.shape, F32)

    qb = qs[...].astype(BF16)
    qrb = qrs[...].astype(BF16)

    def update(k, kr, mask):
        s = _dot_nt(qb, k) + _dot_nt(qrb, kr)
        if mask is not None:
            s = jnp.where(mask, s, NEG_INF)
        m_prev = m_ref[...]
        m_new = jnp.maximum(m_prev, jnp.max(s, axis=-1, keepdims=True))
        p = jnp.exp(s - m_new)
        alpha = jnp.exp(m_prev - m_new)
        l_ref[...] = alpha * l_ref[...] + jnp.sum(p, axis=-1, keepdims=True)
        acc_ref[...] = alpha * acc_ref[...] + _dot(p.astype(BF16), k)
        m_ref[...] = m_new

    for u in range(pages):
        kcat[u * page:(u + 1) * page, :] = lat_refs[u][...].astype(BF16)
        krcat[u * page:(u + 1) * page, :] = kr_refs[u][...].astype(BF16)
    update(kcat[...], krcat[...], None)

    @pl.when(j == npg - 1)
    def _():
        npad = knp.shape[0]
        tok = lax.broadcasted_iota(jnp.int32, (nrow, npad), 0) % ts
        col = lax.broadcasted_iota(jnp.int32, (nrow, npad), 1)
        update(knp[...].astype(BF16), krnp[...].astype(BF16), col <= tok)
        o = acc_ref[...] / l_ref[...]
        for h in range(MLA_HEADS):
            oh = o[h * ts:(h + 1) * ts, :].astype(BF16)
            o_ref[:, h * MLA_V:(h + 1) * MLA_V] = _dot(oh, wuv_ref[h])


def _attn_sample(ql, qr, lat, kr, cache_lat, cache_kr, page_table, layer, wuv, row0, nseq, ts, pages_pref=16):
    n_pages = page_table.shape[1]
    page = cache_lat.shape[2]
    pages = _tile(n_pages, pages_pref, 1)
    npg = n_pages // pages
    blk0 = row0 // ts
    nrow = MLA_HEADS * ts
    kern = functools.partial(_attn_sample_kernel, pages=pages, npg=npg, ts=ts, page=page)

    def page_map(u):
        return lambda b, j, pt: (layer, pt[b, j * pages + u], 0, 0)

    in_specs = [pl.BlockSpec((ts, MLA_HEADS * MLA_KV_RANK), lambda b, j, pt: (blk0 + b, 0)),
                pl.BlockSpec((ts, MLA_HEADS * MLA_ROPE), lambda b, j, pt: (blk0 + b, 0)),
                pl.BlockSpec((ts, MLA_KV_RANK), lambda b, j, pt: (blk0 + b, 0)),
                pl.BlockSpec((ts, MLA_ROPE), lambda b, j, pt: (blk0 + b, 0)),
                pl.BlockSpec(wuv.shape, lambda b, j, pt: (0, 0, 0))]
    in_specs += [pl.BlockSpec((None, None, page, MLA_KV_RANK), page_map(u)) for u in range(pages)]
    in_specs += [pl.BlockSpec((None, None, page, MLA_ROPE), page_map(u)) for u in range(pages)]
    grid_spec = pltpu.PrefetchScalarGridSpec(
        num_scalar_prefetch=1,
        grid=(nseq, npg),
        in_specs=in_specs,
        out_specs=pl.BlockSpec((ts, MLA_HEADS * MLA_V), lambda b, j, pt: (b, 0)),
        scratch_shapes=[pltpu.VMEM((nrow, MLA_KV_RANK), F32),
                        pltpu.VMEM((nrow, MLA_ROPE), F32),
                        pltpu.VMEM((pages * page, MLA_KV_RANK), BF16),
                        pltpu.VMEM((pages * page, MLA_ROPE), BF16),
                        pltpu.VMEM((LANE, MLA_KV_RANK), F32),
                        pltpu.VMEM((LANE, MLA_ROPE), F32),
                        pltpu.VMEM((nrow, 1), F32),
                        pltpu.VMEM((nrow, 1), F32),
                        pltpu.VMEM((nrow, MLA_KV_RANK), F32)])
    return pl.pallas_call(
        kern,
        out_shape=jax.ShapeDtypeStruct((nseq * ts, MLA_HEADS * MLA_V), F32),
        grid_spec=grid_spec,
        compiler_params=_params("arbitrary", "arbitrary"),
        name="attn_sample",
    )(page_table, ql, qr, lat, kr, wuv, *([cache_lat] * pages), *([cache_kr] * pages))


def _mamba_kernel(z_ref, xbc_ref, misc_ref, cbuf_ref, s0_ref, cw_ref, cbias_ref, dtb_ref, alog_ref, dpar_ref,
                  nw_ref, y_ref, sfin_ref, xp_ref, s_ref, ysc_ref, *, L, nc):
    c = pl.program_id(1)
    pad = 8

    @pl.when(c == 0)
    def _():
        xp_ref[pad - 3:pad, :] = cbuf_ref[...]
        s_ref[...] = s0_ref[...]

    x = xbc_ref[...]
    xp_ref[pad:pad + L, :] = x
    cw = cw_ref[...]
    y = (cbias_ref[...] + cw[3:4] * x + cw[2:3] * xp_ref[pad - 1:pad - 1 + L, :]
         + cw[1:2] * xp_ref[pad - 2:pad - 2 + L, :] + cw[0:1] * xp_ref[pad - 3:pad - 3 + L, :])
    tail = xp_ref[pad + L - 3:pad + L, :]
    xp_ref[pad - 3:pad, :] = tail
    xa = _silu(y)
    bm = xa[:, SSM_INNER:SSM_INNER + SSM_GROUPS * SSM_STATE]
    cm = xa[:, SSM_INNER + SSM_GROUPS * SSM_STATE:]
    dt = jax.nn.softplus(misc_ref[:, LANE:2 * LANE] + dtb_ref[...])
    da = dt * (-jnp.exp(alog_ref[...]))
    ri = lax.broadcasted_iota(jnp.int32, (L, L), 0)
    ci = lax.broadcasted_iota(jnp.int32, (L, L), 1)
    causal = ri >= ci
    acs = _dot(causal.astype(F32), da, HI)
    acs_t = acs.T
    eacs = jnp.exp(acs)
    last = acs[L - 1:L, :]
    dte = jnp.exp(last - acs)
    elast = jnp.exp(last)
    dpar = dpar_ref[...]
    hg = SSM_HEADS // SSM_GROUPS
    for g in range(SSM_GROUPS):
        bg = bm[:, g * SSM_STATE:(g + 1) * SSM_STATE]
        cg = cm[:, g * SSM_STATE:(g + 1) * SSM_STATE]
        cbm = _dot_nt(cg, bg, HI)
        for hh in range(hg):
            h = g * hg + hh
            seg = jnp.where(causal, jnp.exp(acs[:, h:h + 1] - acs_t[h:h + 1, :]), 0.0)
            xh = xa[:, h * SSM_HEAD_DIM:(h + 1) * SSM_HEAD_DIM]
            xdt = xh * dt[:, h:h + 1]
            s_old = s_ref[h]
            yh = (_dot(cbm * seg, xdt, HI) + eacs[:, h:h + 1] * _dot_nt(cg, s_old, HI)
                  + dpar[:, h:h + 1] * xh)
            ysc_ref[:, h * SSM_HEAD_DIM:(h + 1) * SSM_HEAD_DIM] = yh
            s_ref[h] = elast[:, h:h + 1] * s_old + _dot_tn(xdt * dte[:, h:h + 1], bg, HI)
    yv = ysc_ref[...] * _silu(z_ref[...])
    gw = SSM_INNER // SSM_GROUPS
    nw = nw_ref[...]
    for g in range(SSM_GROUPS):
        yg = yv[:, g * gw:(g + 1) * gw]
        y_ref[:, g * gw:(g + 1) * gw] = _rms(yg, nw[:, g * gw:(g + 1) * gw])

    @pl.when(c == nc - 1)
    def _():
        sfin_ref[...] = s_ref[...]


def _mamba(p, cbuf, s0, cw, cbias, dtb, alog, dpar, nw, row0, nseq, t):
    L = min(SSM_CHUNK, t)
    nc = t // L
    blk0 = row0 // L
    kern = functools.partial(_mamba_kernel, L=L, nc=nc)
    vec = lambda w: pl.BlockSpec((1, w), lambda b, c: (0, 0))
    return pl.pallas_call(
        kern,
        out_shape=[jax.ShapeDtypeStruct((nseq * t, SSM_INNER), F32),
                   jax.ShapeDtypeStruct((nseq, SSM_HEADS, SSM_HEAD_DIM, SSM_STATE), F32)],
        grid=(nseq, nc),
        in_specs=[pl.BlockSpec((L, SSM_INNER), lambda b, c: (blk0 + b * nc + c, COL_Z // SSM_INNER)),
                  pl.BlockSpec((L, SSM_CONV_DIM), lambda b, c: (blk0 + b * nc + c, COL_XBC // SSM_CONV_DIM)),
                  pl.BlockSpec((L, MISC_W), lambda b, c: (blk0 + b * nc + c, COL_MISC // MISC_W)),
                  pl.BlockSpec((None, SSM_CONV - 1, SSM_CONV_DIM), lambda b, c: (b, 0, 0)),
                  pl.BlockSpec((None, SSM_HEADS, SSM_HEAD_DIM, SSM_STATE), lambda b, c: (b, 0, 0, 0)),
                  pl.BlockSpec((SSM_CONV, SSM_CONV_DIM), lambda b, c: (0, 0)),
                  vec(SSM_CONV_DIM), vec(LANE), vec(LANE), vec(LANE), vec(SSM_INNER)],
        out_specs=[pl.BlockSpec((L, SSM_INNER), lambda b, c: (b * nc + c, 0)),
                   pl.BlockSpec((None, SSM_HEADS, SSM_HEAD_DIM, SSM_STATE), lambda b, c: (b, 0, 0, 0))],
        scratch_shapes=[pltpu.VMEM((L + 8, SSM_CONV_DIM), F32),
                        pltpu.VMEM((SSM_HEADS, SSM_HEAD_DIM, SSM_STATE), F32),
                        pltpu.VMEM((L, SSM_INNER), F32)],
        compiler_params=_params("arbitrary", "arbitrary"),
        name="ssd_scan",
    )(p, p, p, cbuf, s0, cw, cbias, dtb, alog, dpar, nw)


def _gla_kernel(gq_ref, gk_ref, gv_ref, og_ref, misc_ref, s0_ref, wg_ref, bg_ref, gn_ref, o_ref, sfin_ref,
                st_ref, gl_ref, *, L, ls, nc):
    c = pl.program_id(1)

    @pl.when(c == 0)
    def _():
        for h in range(GLA_HEADS):
            st_ref[h] = s0_ref[h].T

    gl_ref[...] = jax.nn.log_sigmoid(_dot(misc_ref[:, LANE:2 * LANE], wg_ref[...], HI) + bg_ref[...]) \
        * (1.0 / GLA_GATE_NORMALIZER)
    ri = lax.broadcasted_iota(jnp.int32, (ls, ls), 0)
    ci = lax.broadcasted_iota(jnp.int32, (ls, ls), 1)
    tril = (ri >= ci).astype(F32)
    rows = lax.broadcasted_iota(jnp.int32, (ls, GLA_DK), 0)
    gn = gn_ref[...]

    def sub_chunk(sidx, carry):
        r0 = pl.multiple_of(sidx * ls, ls)
        for h in range(GLA_HEADS):
            q = gq_ref[pl.ds(r0, ls), h * GLA_DK:(h + 1) * GLA_DK] * (GLA_DK ** -0.5)
            k = gk_ref[pl.ds(r0, ls), h * GLA_DK:(h + 1) * GLA_DK]
            v = gv_ref[pl.ds(r0, ls), h * GLA_DV:(h + 1) * GLA_DV]
            g = gl_ref[pl.ds(r0, ls), h * GLA_DK:(h + 1) * GLA_DK]
            bcs = _dot(tril, g, HI)
            st = st_ref[h]
            o = _dot_nt(q * jnp.exp(bcs), st, HI)
            for j in range(ls):
                dj = jnp.where(rows >= j, bcs - bcs[j:j + 1, :], NEG_INF)
                w = jnp.sum(q * k[j:j + 1, :] * jnp.exp(dj), axis=-1, keepdims=True)
                o = o + w * v[j:j + 1, :]
            lastb = bcs[ls - 1:ls, :]
            st_ref[h] = st * jnp.exp(lastb) + _dot_tn(v, k * jnp.exp(lastb - bcs), HI)
            o = o * lax.rsqrt(jnp.mean(o * o, axis=-1, keepdims=True) + EPS) * gn
            o = o * _silu(og_ref[pl.ds(r0, ls), h * GLA_DV:(h + 1) * GLA_DV])
            o_ref[pl.ds(r0, ls), h * GLA_DV:(h + 1) * GLA_DV] = o
        return carry

    lax.fori_loop(0, L // ls, sub_chunk, 0)

    @pl.when(c == nc - 1)
    def _():
        for h in range(GLA_HEADS):
            sfin_ref[h] = st_ref[h].T


def _gla(p, s0, wg, bg, gn, row0, nseq, t):
    L = min(128, t)
    ls = min(GLA_SUB, L)
    nc = t // L
    blk0 = row0 // L
    kern = functools.partial(_gla_kernel, L=L, ls=ls, nc=nc)
    rowblk = lambda w, col: pl.BlockSpec((L, w), lambda b, c: (blk0 + b * nc + c, col // w))
    return pl.pallas_call(
        kern,
        out_shape=[jax.ShapeDtypeStruct((nseq * t, GLA_VALUE), F32),
                   jax.ShapeDtypeStruct((nseq, GLA_HEADS, GLA_DK, GLA_DV), F32)],
        grid=(nseq, nc),
        in_specs=[rowblk(GLA_KEY, COL_GQ), rowblk(GLA_KEY, COL_GK), rowblk(GLA_VALUE, COL_GV),
                  rowblk(GLA_VALUE, COL_OG), rowblk(MISC_W, COL_MISC),
                  pl.BlockSpec((None, GLA_HEADS, GLA_DK, GLA_DV), lambda b, c: (b, 0, 0, 0)),
                  pl.BlockSpec((LANE, GLA_KEY), lambda b, c: (0, 0)),
                  pl.BlockSpec((1, GLA_KEY), lambda b, c: (0, 0)),
                  pl.BlockSpec((1, GLA_DV), lambda b, c: (0, 0))],
        out_specs=[pl.BlockSpec((L, GLA_VALUE), lambda b, c: (b * nc + c, 0)),
                   pl.BlockSpec((None, GLA_HEADS, GLA_DK, GLA_DV), lambda b, c: (b, 0, 0, 0))],
        scratch_shapes=[pltpu.VMEM((GLA_HEADS, GLA_DV, GLA_DK), F32),
                        pltpu.VMEM((L, GLA_KEY), F32)],
        compiler_params=_params("arbitrary", "arbitrary"),
        name="gla_scan",
    )(p, p, p, p, p, s0, wg, bg, gn)


def _merge_kernel(oa_ref, ob_ref, oc_ref, ga_ref, gb_ref, gc_ref, wb_ref, o_ref):
    acc = jax.nn.sigmoid(ga_ref[...]) * _dot(oa_ref[...].astype(BF16), wb_ref[0])
    acc += jax.nn.sigmoid(gb_ref[...]) * _dot(ob_ref[...].astype(BF16), wb_ref[1])
    acc += jax.nn.sigmoid(gc_ref[...]) * _dot(oc_ref[...].astype(BF16), wb_ref[2])
    o_ref[...] = acc.astype(BF16)


def _merge(oa, ob, oc, p, wb, tm_pref=512, tn=512):
    rows = oa.shape[0]
    tm = _tile(rows, tm_pref, 16)
    nj = D_MODEL // tn
    br = lambda: pl.BlockSpec((tm, BRANCH_W), lambda j, i: (i, 0))
    gate = lambda n: pl.BlockSpec((tm, tn), lambda j, i: (i, (COL_GATES + n * D_MODEL) // tn + j))
    return pl.pallas_call(
        _merge_kernel,
        out_shape=jax.ShapeDtypeStruct((rows, D_MODEL), BF16),
        grid=(nj, rows // tm),
        in_specs=[br(), br(), br(), gate(0), gate(1), gate(2),
                  pl.BlockSpec((3, BRANCH_W, tn), lambda j, i: (0, 0, j))],
        out_specs=pl.BlockSpec((tm, tn), lambda j, i: (i, j)),
        compiler_params=_params("arbitrary", "arbitrary"),
        name="branch_merge",
    )(oa, ob, oc, p, p, p, wb)


def _out_proj_kernel(m_ref, w_ref, x_ref, o_ref):
    o_ref[...] = x_ref[...] + _dot(m_ref[...], w_ref[...])


def _out_proj(m, w, x, tm_pref=512, tn=512):
    rows, k = m.shape
    n = w.shape[1]
    tm = _tile(rows, tm_pref, 16)
    return pl.pallas_call(
        _out_proj_kernel,
        out_shape=jax.ShapeDtypeStruct((rows, n), F32),
        grid=(n // tn, rows // tm),
        in_specs=[pl.BlockSpec((tm, k), lambda j, i: (i, 0)),
                  pl.BlockSpec((k, tn), lambda j, i: (0, j)),
                  pl.BlockSpec((tm, tn), lambda j, i: (i, j))],
        out_specs=pl.BlockSpec((tm, tn), lambda j, i: (i, j)),
        compiler_params=_params("arbitrary", "arbitrary"),
        name="out_proj",
    )(m, w, x)


def _top16(s):
    n = s.shape[0]
    iota = lax.broadcasted_iota(jnp.int32, s.shape, 0)
    rank = jnp.full(s.shape, PEER_TOPK, jnp.int32)
    vals = []
    for r in range(PEER_TOPK):
        m = jnp.max(s, axis=0, keepdims=True)
        pos = jnp.min(jnp.where(s == m, iota, n), axis=0, keepdims=True)
        hit = iota == pos
        rank = jnp.where(hit, r, rank)
        s = jnp.where(hit, NEG_INF, s)
        vals.append(m)
    return rank, vals


def _peer_route_kernel(x_ref, g_ref, wqt_ref, sk_ref, hq_ref, rank2_ref, cnt_ref, a_ref, b_ref):
    hq = _rms(x_ref[...], g_ref[...]).astype(BF16)
    hq_ref[...] = hq
    qt = _dot_nt(wqt_ref[...], hq)
    t = qt.shape[1]
    iota_c = lax.broadcasted_iota(jnp.int32, (PEER_TOPK * PEER_TOPK, t), 0)
    iota_r = lax.broadcasted_iota(jnp.int32, (PEER_TOPK, t), 0)
    for h in range(PEER_HEADS):
        base = h * 2 * PEER_HALF
        s1 = _dot(sk_ref[0, h], qt[base:base + PEER_HALF].astype(BF16))
        s2 = _dot(sk_ref[1, h], qt[base + PEER_HALF:base + 2 * PEER_HALF].astype(BF16))
        rank1, v1 = _top16(s1)
        rank2, v2 = _top16(s2)
        v2s = jnp.concatenate(v2, axis=0)
        cand = jnp.concatenate([v1[r] + v2s for r in range(PEER_TOPK)], axis=0)
        top = v1[0] + v2[0]
        cnt = jnp.zeros((PEER_TOPK, t), F32)
        z = jnp.zeros((1, t), F32)
        for _ in range(PEER_TOPK):
            m = jnp.max(cand, axis=0, keepdims=True)
            pos = jnp.min(jnp.where(cand == m, iota_c, PEER_TOPK * PEER_TOPK), axis=0, keepdims=True)
            z = z + jnp.exp(m - top)
            cnt = cnt + (iota_r == jnp.right_shift(pos, 4)).astype(F32)
            cand = jnp.where(iota_c == pos, NEG_INF, cand)
        cntrow = jnp.zeros(s1.shape, F32)
        for r in range(PEER_TOPK):
            cntrow = jnp.where(rank1 == r, cnt[r:r + 1, :], cntrow)
        rank2_ref[h] = rank2.astype(F32).astype(BF16)
        cnt_ref[h] = cntrow
        a_ref[h] = jnp.where(rank1 < PEER_TOPK, jnp.exp(s1 - v1[0]), 0.0) / z
        b_ref[h] = jnp.where(rank2 < PEER_TOPK, jnp.exp(s2 - v2[0]), 0.0).astype(BF16)


def _peer_route(x, g, wqt, sk, tt):
    rows, d = x.shape
    tab = jax.ShapeDtypeStruct((PEER_HEADS, PEER_NKEYS, rows), F32)
    tab16 = jax.ShapeDtypeStruct((PEER_HEADS, PEER_NKEYS, rows), BF16)
    tab_spec = pl.BlockSpec((PEER_HEADS, PEER_NKEYS, tt), lambda i: (0, 0, i))
    return pl.pallas_call(
        _peer_route_kernel,
        out_shape=[jax.ShapeDtypeStruct((rows, d), BF16), tab16, tab, tab, tab16],
        grid=(rows // tt,),
        in_specs=[pl.BlockSpec((tt, d), lambda i: (i, 0)),
                  pl.BlockSpec((1, d), lambda i: (0, 0)),
                  pl.BlockSpec(wqt.shape, lambda i: (0, 0)),
                  pl.BlockSpec(sk.shape, lambda i: (0, 0, 0, 0))],
        out_specs=[pl.BlockSpec((tt, d), lambda i: (i, 0)), tab_spec, tab_spec, tab_spec, tab_spec],
        compiler_params=_params("arbitrary"),
        name="peer_route",
    )(x, g, wqt, sk)


def _peer_dense_kernel(hq_ref, x_ref, u_ref, v_ref, rank2_ref, cnt_ref, a_ref, b_ref, o_ref, acc_ref, wa_ref,
                       *, eb, ne, half):
    e = pl.program_id(1)

    @pl.when(e == 0)
    def _():
        acc_ref[...] = jnp.zeros(acc_ref.shape, F32)

    nsub = half // PEER_NKEYS
    for hf in range(eb // half):
        act = _dot_nt(u_ref[hf * half:(hf + 1) * half, :], hq_ref[...])
        act = 0.5 * act * (1.0 + lax.erf(act * (2.0 ** -0.5)))
        for ii in range(nsub):
            i = hf * nsub + ii
            w = None
            for h in range(PEER_HEADS):
                c = cnt_ref[h, i:i + 1, :].astype(BF16)
                av = a_ref[h, i:i + 1, :].astype(BF16)
                term = av * jnp.where(rank2_ref[h] < c, b_ref[h], jnp.zeros((), BF16))
                w = term if w is None else w + term
            r0 = hf * half + ii * PEER_NKEYS
            wa_ref[r0:r0 + PEER_NKEYS, :] = w * act[ii * PEER_NKEYS:(ii + 1) * PEER_NKEYS].astype(BF16)
    acc_ref[...] += _dot_tn(wa_ref[...], v_ref[...])

    @pl.when(e == ne - 1)
    def _():
        o_ref[...] = x_ref[...] + acc_ref[...]


def _peer_dense(hq, x, u, v, rank2, cntrow, a, b, tt, eb=1024, half=256):
    rows, d = x.shape
    n_exp = u.shape[0]
    ne = n_exp // eb
    kern = functools.partial(_peer_dense_kernel, eb=eb, ne=ne, half=half)
    tab_spec = pl.BlockSpec((PEER_HEADS, PEER_NKEYS, tt), lambda i, e: (0, 0, i))
    row_spec = pl.BlockSpec((PEER_HEADS, eb // PEER_NKEYS, tt), lambda i, e: (0, e, i))
    return pl.pallas_call(
        kern,
        out_shape=jax.ShapeDtypeStruct((rows, d), F32),
        grid=(rows // tt, ne),
        in_specs=[pl.BlockSpec((tt, d), lambda i, e: (i, 0)),
                  pl.BlockSpec((tt, d), lambda i, e: (i, 0)),
                  pl.BlockSpec((eb, d), lambda i, e: (e, 0)),
                  pl.BlockSpec((eb, d), lambda i, e: (e, 0)),
                  tab_spec, row_spec, row_spec, tab_spec],
        out_specs=pl.BlockSpec((tt, d), lambda i, e: (i, 0)),
        scratch_shapes=[pltpu.VMEM((tt, d), F32), pltpu.VMEM((eb, tt), BF16)],
        compiler_params=_params("arbitrary", "arbitrary"),
        name="peer_dense",
    )(hq, x, u, v, rank2, cntrow, a, b)


def _final_norm_kernel(x_ref, g_ref, o_ref):
    o_ref[...] = _rms(x_ref[...], g_ref[...])


def _final_norm(x, g, tm_pref=512):
    rows, d = x.shape
    tm = _tile(rows, tm_pref)
    return pl.pallas_call(
        _final_norm_kernel,
        out_shape=jax.ShapeDtypeStruct((rows, d), F32),
        grid=(rows // tm,),
        in_specs=[pl.BlockSpec((tm, d), lambda i: (i, 0)), pl.BlockSpec((1, d), lambda i: (0, 0))],
        out_specs=pl.BlockSpec((tm, d), lambda i: (i, 0)),
        compiler_params=_params("arbitrary"),
        name="final_norm",
    )(x, g)


def _pad_lanes(v, width=LANE):
    return jnp.pad(v, [(0, 0)] * (v.ndim - 1) + [(0, width - v.shape[-1])])


def _rot_half(w):
    half = w.shape[-1] // 2
    return jnp.concatenate([w[..., half:], w[..., :half]], axis=-1)


def _layout_w_in(w_in):
    widths = (MLA_Q_RANK, MLA_KV_RANK, MLA_ROPE, SSM_INNER, SSM_CONV_DIM, SSM_HEADS,
              GLA_KEY, GLA_KEY, GLA_VALUE, GLA_GATE_RANK, GLA_VALUE, 3 * D_MODEL)
    offs = [0]
    for w in widths:
        offs.append(offs[-1] + w)
    cq, ckv, kr, z, xbc, dt, gq, gk, gv, glr, og, gates = [w_in[..., offs[i]:offs[i + 1]] for i in range(len(widths))]
    zero = jnp.zeros(w_in.shape[:-1] + (MISC_W - 2 * MLA_ROPE - SSM_HEADS - GLA_GATE_RANK,), w_in.dtype)
    out = jnp.concatenate([gates, z, og, gv, xbc, gq, gk, cq, ckv, kr, _rot_half(kr), dt, glr, zero], axis=-1)
    return out.astype(BF16)


def _rope_tables(pos):
    half = MLA_ROPE // 2
    inv = ROPE_BASE ** (-jnp.arange(half, dtype=F32) / half)
    ang = pos.astype(F32)[:, None] * inv[None, :]
    cos, sin = jnp.cos(ang), jnp.sin(ang)
    cos_t = jnp.tile(jnp.concatenate([cos, cos], axis=-1), (1, MLA_HEADS))
    sin_t = jnp.tile(jnp.concatenate([-sin, sin], axis=-1), (1, MLA_HEADS))
    return cos_t, sin_t


def kernel(x_prompt, x_sample, cache_mla_latent, cache_mla_krope, page_table, state_ssm_conv, state_ssm, state_gla, norm_mix, w_in, mla_q_norm, mla_w_uq, mla_kv_norm, mla_w_uk, mla_w_uv, ssm_conv_w, ssm_conv_b, ssm_dt_bias, ssm_a_log, ssm_d, ssm_norm, gla_gate_w, gla_gate_b, gla_norm, w_branch, w_out, norm_ffn, peer_wq, peer_subkeys, peer_u, peer_v, norm_final):
    bp, tp, d = x_prompt.shape
    bs, ts, _ = x_sample.shape
    depth = w_in.shape[0]
    n_pages = page_table.shape[1]
    past_len = n_pages * cache_mla_latent.shape[2]
    rows_p = bp * tp
    rows_s = bs * ts
    rows = rows_p + rows_s

    x = jnp.concatenate([x_prompt.reshape(rows_p, d), x_sample.reshape(rows_s, d)], axis=0)
    pos = jnp.concatenate([jnp.tile(jnp.arange(tp), bp), jnp.tile(past_len + jnp.arange(ts), bs)])
    cos_t, sin_t = _rope_tables(pos)
    w_in_l = _layout_w_in(w_in)
    uq_nope = mla_w_uq[..., :MLA_NOPE].reshape(depth, MLA_Q_RANK, MLA_HEADS * MLA_NOPE)
    uq_rope = mla_w_uq[..., MLA_NOPE:]
    wq_all = jnp.concatenate([uq_nope,
                              uq_rope.reshape(depth, MLA_Q_RANK, MLA_HEADS * MLA_ROPE),
                              _rot_half(uq_rope).reshape(depth, MLA_Q_RANK, MLA_HEADS * MLA_ROPE)],
                             axis=-1).astype(BF16)
    wuk_t = jnp.transpose(mla_w_uk, (0, 2, 3, 1)).astype(BF16)
    wuv = jnp.transpose(mla_w_uv, (0, 2, 1, 3)).astype(BF16)
    wuvt = jnp.transpose(mla_w_uv, (0, 2, 3, 1)).astype(BF16)
    dtb = _pad_lanes(ssm_dt_bias)
    alog = _pad_lanes(ssm_a_log)
    dpar = _pad_lanes(ssm_d)
    wg = jnp.zeros((depth, LANE, GLA_KEY), F32).at[:, SSM_HEADS:SSM_HEADS + GLA_GATE_RANK, :].set(gla_gate_w)
    wb = w_branch.astype(BF16)
    wo = w_out.astype(BF16)
    wqt = jnp.swapaxes(peer_wq, 1, 2).astype(BF16)
    sk = peer_subkeys.astype(BF16)
    ub = peer_u.astype(BF16)
    vb = peer_v.astype(BF16)
    conv0 = jnp.zeros((bp, SSM_CONV - 1, SSM_CONV_DIM), F32)
    ssm0 = jnp.zeros((bp, SSM_HEADS, SSM_HEAD_DIM, SSM_STATE), F32)
    gla0 = jnp.zeros((bp, GLA_HEADS, GLA_DK, GLA_DV), F32)
    tt_route = _tile(rows, 256, LANE)
    tt_dense = _tile(rows, 512, LANE)

    new_p = [[], [], [], [], []]
    new_s = [[], [], [], [], []]
    for l in range(depth):
        p = _norm_matmul(x, norm_mix[l][None], w_in_l[l])
        ql, qr, lat, kr = _mla_prep(p, cos_t, sin_t, mla_q_norm[l][None], mla_kv_norm[l][None], wq_all[l], wuk_t[l])
        oa_p = _attn_prompt(ql, qr, lat, kr, wuvt[l], bp, tp)
        oa_s = _attn_sample(ql, qr, lat, kr, cache_mla_latent, cache_mla_krope, page_table, l, wuv[l],
                            rows_p, bs, ts)
        ssm_args = (ssm_conv_w[l], ssm_conv_b[l][None], dtb[l][None], alog[l][None], dpar[l][None],
                    ssm_norm[l][None])
        ob_p, ssm_p = _mamba(p, conv0, ssm0, *ssm_args, 0, bp, tp)
        ob_s, ssm_s = _mamba(p, state_ssm_conv[l], state_ssm[l], *ssm_args, rows_p, bs, ts)
        gla_args = (wg[l], gla_gate_b[l][None], gla_norm[l][None])
        oc_p, gla_p = _gla(p, gla0, *gla_args, 0, bp, tp)
        oc_s, gla_s = _gla(p, state_gla[l], *gla_args, rows_p, bs, ts)
        merged = _merge(jnp.concatenate([oa_p, oa_s]), jnp.concatenate([ob_p, ob_s]),
                        jnp.concatenate([oc_p, oc_s]), p, wb[l])
        x = _out_proj(merged, wo[l], x)
        hq, rank2, cntrow, ga, gb = _peer_route(x, norm_ffn[l][None], wqt[l], sk[l], tt_route)
        x = _peer_dense(hq, x, ub[l], vb[l], rank2, cntrow, ga, gb, tt_dense)

        xbc = p[:, COL_XBC:COL_XBC + SSM_CONV_DIM]
        conv_p = xbc[:rows_p].reshape(bp, tp, SSM_CONV_DIM)[:, tp - (SSM_CONV - 1):]
        conv_s = xbc[rows_p:].reshape(bs, ts, SSM_CONV_DIM)[:, ts - (SSM_CONV - 1):]
        for lst, a in zip(new_p, (lat[:rows_p].reshape(bp, tp, MLA_KV_RANK), kr[:rows_p].reshape(bp, tp, MLA_ROPE),
                                  conv_p, ssm_p, gla_p)):
            lst.append(a)
        for lst, a in zip(new_s, (lat[rows_p:].reshape(bs, ts, MLA_KV_RANK), kr[rows_p:].reshape(bs, ts, MLA_ROPE),
                                  conv_s, ssm_s, gla_s)):
            lst.append(a)

    y = _final_norm(x, norm_final[None])
    y_prompt = y[:rows_p].reshape(bp, tp, d)
    y_sample = y[rows_p:].reshape(bs, ts, d)
    outs_p = [jnp.stack(a) for a in new_p]
    outs_s = [jnp.stack(a) for a in new_s]
    return (y_prompt, y_sample, *outs_p, *outs_s)
```

```python
import functools
import math

import jax
import jax.numpy as jnp
from jax import lax
from jax.experimental import pallas as pl
from jax.experimental.pallas import tpu as pltpu

F32 = jnp.float32
BF16 = jnp.bfloat16
HI = lax.Precision.HIGHEST
NEG_INF = float("-inf")

EPS = 1e-6
D_MODEL = 2048
BRANCH_W = 1024
MLA_HEADS = 8
MLA_NOPE = 128
MLA_ROPE = 64
MLA_V = 128
MLA_Q_RANK = 512
MLA_KV_RANK = 256
MLA_SCALE = (MLA_NOPE + MLA_ROPE) ** -0.5
ROPE_BASE = 10000.0
SSM_INNER = 1024
SSM_HEAD_DIM = 64
SSM_HEADS = 16
SSM_GROUPS = 2
SSM_STATE = 128
SSM_CONV = 4
SSM_CONV_DIM = SSM_INNER + 2 * SSM_GROUPS * SSM_STATE
SSM_CHUNK = 128
GLA_HEADS = 4
GLA_DK = 128
GLA_DV = 256
GLA_KEY = 512
GLA_VALUE = 1024
GLA_GATE_RANK = 16
GLA_GATE_NORMALIZER = 16.0
GLA_SUB = 16
PEER_HEADS = 8
PEER_NKEYS = 128
PEER_TOPK = 16
PEER_HALF = 128

LANE = 128
VMEM_LIMIT = 56 * 1024 * 1024

COL_GATES = 0
COL_Z = 6144
COL_OG = 7168
COL_GV = 8192
COL_XBC = 9216
COL_GQ = 10752
COL_GK = 11264
COL_CQ = 11776
COL_CKV = 12288
COL_MISC = 12544
IN_COLS = 12800
MISC_W = 256


def _tile(n, pref, mult=8):
    for t in range(min(n, pref), 0, -1):
        if n % t == 0 and t % mult == 0:
            return t
    return n


def _params(*sem):
    return pltpu.CompilerParams(dimension_semantics=sem, vmem_limit_bytes=VMEM_LIMIT)


def _rms(x, g):
    return x * lax.rsqrt(jnp.mean(x * x, axis=-1, keepdims=True) + EPS) * g


def _dot(a, b, prec=None):
    return jnp.dot(a, b, preferred_element_type=F32, precision=prec)


def _dot_nt(a, b, prec=None):
    return lax.dot_general(a, b, (((1,), (1,)), ((), ())), preferred_element_type=F32, precision=prec)


def _dot_tn(a, b, prec=None):
    return lax.dot_general(a, b, (((0,), (0,)), ((), ())), preferred_element_type=F32, precision=prec)


def _silu(x):
    return x * jax.nn.sigmoid(x)


def _norm_matmul_kernel(x_ref, g_ref, w_ref, o_ref, h_ref):
    @pl.when(pl.program_id(1) == 0)
    def _():
        h_ref[...] = _rms(x_ref[...], g_ref[...]).astype(BF16)

    o_ref[...] = _dot(h_ref[...], w_ref[...])


def _norm_matmul(x, g, w, layer, tm_pref=1024, tn=512):
    m, k = x.shape
    n = w.shape[2]
    tm = _tile(m, tm_pref)
    return pl.pallas_call(
        _norm_matmul_kernel,
        out_shape=jax.ShapeDtypeStruct((m, n), F32),
        grid=(m // tm, n // tn),
        in_specs=[pl.BlockSpec((tm, k), lambda i, j: (i, 0)),
                  pl.BlockSpec((1, k), lambda i, j: (0, 0)),
                  pl.BlockSpec((None, k, tn), lambda i, j: (layer, 0, j))],
        out_specs=pl.BlockSpec((tm, tn), lambda i, j: (i, j)),
        scratch_shapes=[pltpu.VMEM((tm, k), BF16)],
        compiler_params=_params("arbitrary", "arbitrary"),
        name="norm_in_proj",
    )(x, g, w)


def _mla_prep_kernel(cq_ref, ckv_ref, misc_ref, cos_ref, sin_ref, qn_ref, kvn_ref, wq_ref, wuk_ref,
                     ql_ref, qr_ref, lat_ref, kr_ref):
    cqn = _rms(cq_ref[...], qn_ref[...]).astype(BF16)
    q = _dot(cqn, wq_ref[...])
    cos = cos_ref[...]
    sin = sin_ref[...]
    nope_w = MLA_HEADS * MLA_NOPE
    rope_w = MLA_HEADS * MLA_ROPE
    qr_ref[...] = (q[:, nope_w:nope_w + rope_w] * cos + q[:, nope_w + rope_w:] * sin) * MLA_SCALE
    for h in range(MLA_HEADS):
        qh = q[:, h * MLA_NOPE:(h + 1) * MLA_NOPE].astype(BF16)
        ql_ref[:, h * MLA_KV_RANK:(h + 1) * MLA_KV_RANK] = _dot(qh, wuk_ref[h]) * MLA_SCALE
    lat_ref[...] = _rms(ckv_ref[...], kvn_ref[...])
    misc = misc_ref[...]
    kr_ref[...] = misc[:, 0:MLA_ROPE] * cos[:, 0:MLA_ROPE] + misc[:, MLA_ROPE:2 * MLA_ROPE] * sin[:, 0:MLA_ROPE]


def _mla_prep(p, cos, sin, qn, kvn, wq, wuk, tm_pref=256):
    rows = p.shape[0]
    tm = _tile(rows, tm_pref)
    lat_w = MLA_HEADS * MLA_KV_RANK
    rope_w = MLA_HEADS * MLA_ROPE
    return pl.pallas_call(
        _mla_prep_kernel,
        out_shape=[jax.ShapeDtypeStruct((rows, lat_w), F32),
                   jax.ShapeDtypeStruct((rows, rope_w), F32),
                   jax.ShapeDtypeStruct((rows, MLA_KV_RANK), F32),
                   jax.ShapeDtypeStruct((rows, MLA_ROPE), F32)],
        grid=(rows // tm,),
        in_specs=[pl.BlockSpec((tm, MLA_Q_RANK), lambda i: (i, COL_CQ // MLA_Q_RANK)),
                  pl.BlockSpec((tm, MLA_KV_RANK), lambda i: (i, COL_CKV // MLA_KV_RANK)),
                  pl.BlockSpec((tm, MISC_W), lambda i: (i, COL_MISC // MISC_W)),
                  pl.BlockSpec((tm, rope_w), lambda i: (i, 0)),
                  pl.BlockSpec((tm, rope_w), lambda i: (i, 0)),
                  pl.BlockSpec((1, MLA_Q_RANK), lambda i: (0, 0)),
                  pl.BlockSpec((1, MLA_KV_RANK), lambda i: (0, 0)),
                  pl.BlockSpec(wq.shape, lambda i: (0, 0)),
                  pl.BlockSpec(wuk.shape, lambda i: (0, 0, 0))],
        out_specs=[pl.BlockSpec((tm, lat_w), lambda i: (i, 0)),
                   pl.BlockSpec((tm, rope_w), lambda i: (i, 0)),
                   pl.BlockSpec((tm, MLA_KV_RANK), lambda i: (i, 0)),
                   pl.BlockSpec((tm, MLA_ROPE), lambda i: (i, 0))],
        compiler_params=_params("arbitrary"),
        name="mla_prep",
    )(p, p, p, cos, sin, qn, kvn, wq, wuk)


def _attn_prompt_kernel(ql_ref, qr_ref, k_ref, kr_ref, wuvt_ref, o_ref, qs, qrs, m_ref, l_ref, acc_ref, *, tq, tk):
    qi = pl.program_id(1)
    ki = pl.program_id(2)
    ncol = MLA_HEADS * tq

    @pl.when(ki == 0)
    def _():
        for h in range(MLA_HEADS):
            qs[h * tq:(h + 1) * tq, :] = ql_ref[:, h * MLA_KV_RANK:(h + 1) * MLA_KV_RANK].astype(BF16)
            qrs[h * tq:(h + 1) * tq, :] = qr_ref[:, h * MLA_ROPE:(h + 1) * MLA_ROPE].astype(BF16)
        m_ref[...] = jnp.full(m_ref.shape, NEG_INF, F32)
        l_ref[...] = jnp.zeros(l_ref.shape, F32)
        acc_ref[...] = jnp.zeros(acc_ref.shape, F32)

    @pl.when(ki <= qi)
    def _():
        k = k_ref[...].astype(BF16)
        kr = kr_ref[...].astype(BF16)
        st = _dot_nt(k, qs[...]) + _dot_nt(kr, qrs[...])
        kpos = ki * tk + lax.broadcasted_iota(jnp.int32, (tk, ncol), 0)
        qpos = qi * tq + (lax.broadcasted_iota(jnp.int32, (tk, ncol), 1) & (tq - 1))
        st = jnp.where(kpos <= qpos, st, NEG_INF)
        m_prev = m_ref[...]
        m_new = jnp.maximum(m_prev, jnp.max(st, axis=0, keepdims=True))
        p = jnp.exp(st - m_new)
        alpha = jnp.exp(m_prev - m_new)
        l_ref[...] = alpha * l_ref[...] + jnp.sum(p, axis=0, keepdims=True)
        acc_ref[...] = alpha * acc_ref[...] + _dot_tn(k, p.astype(BF16))
        m_ref[...] = m_new

    @pl.when(ki == qi)
    def _():
        o = (acc_ref[...] / l_ref[...]).astype(BF16)
        for h in range(MLA_HEADS):
            o_ref[:, h * MLA_V:(h + 1) * MLA_V] = _dot(wuvt_ref[h], o[:, h * tq:(h + 1) * tq]).T


def _attn_prompt(ql, qr, lat, kr, wuvt, nb, t, tq_pref=256):
    tq = _tile(t, tq_pref)
    assert tq & (tq - 1) == 0
    nq = t // tq
    ncol = MLA_HEADS * tq
    kern = functools.partial(_attn_prompt_kernel, tq=tq, tk=tq)
    return pl.pallas_call(
        kern,
        out_shape=jax.ShapeDtypeStruct((nb * t, MLA_HEADS * MLA_V), F32),
        grid=(nb, nq, nq),
        in_specs=[pl.BlockSpec((tq, MLA_HEADS * MLA_KV_RANK), lambda b, qi, ki: (b * nq + qi, 0)),
                  pl.BlockSpec((tq, MLA_HEADS * MLA_ROPE), lambda b, qi, ki: (b * nq + qi, 0)),
                  pl.BlockSpec((tq, MLA_KV_RANK), lambda b, qi, ki: (b * nq + jnp.minimum(ki, qi), 0)),
                  pl.BlockSpec((tq, MLA_ROPE), lambda b, qi, ki: (b * nq + jnp.minimum(ki, qi), 0)),
                  pl.BlockSpec(wuvt.shape, lambda b, qi, ki: (0, 0, 0))],
        out_specs=pl.BlockSpec((tq, MLA_HEADS * MLA_V), lambda b, qi, ki: (b * nq + qi, 0)),
        scratch_shapes=[pltpu.VMEM((ncol, MLA_KV_RANK), BF16),
                        pltpu.VMEM((ncol, MLA_ROPE), BF16),
                        pltpu.VMEM((1, ncol), F32),
                        pltpu.VMEM((1, ncol), F32),
                        pltpu.VMEM((MLA_KV_RANK, ncol), F32)],
        compiler_params=_params("arbitrary", "arbitrary", "arbitrary"),
        name="attn_prompt",
    )(ql, qr, lat, kr, wuvt)


def _attn_sample_kernel(pt_ref, ql_ref, qr_ref, kn_ref, krn_ref, wuv_ref, *rest, pages, npg, ts, page, group):
    n_in = group * pages
    lat_refs = rest[:n_in]
    krt_refs = rest[n_in:2 * n_in]
    o_ref = rest[2 * n_in]
    qs, qrs, kcat, krcat_t, knp, krnp, m_ref, l_ref, acc_ref = rest[2 * n_in + 1:]
    j = pl.program_id(1)
    nrow = MLA_HEADS * ts

    @pl.when(j == 0)
    def _():
        knp[...] = jnp.zeros(knp.shape, F32)
        krnp[...] = jnp.zeros(krnp.shape, F32)
        for g in range(group):
            for h in range(MLA_HEADS):
                qs[g, h * ts:(h + 1) * ts, :] = ql_ref[g * ts:(g + 1) * ts, h * MLA_KV_RANK:(h + 1) * MLA_KV_RANK]
                qrs[g, h * ts:(h + 1) * ts, :] = qr_ref[g * ts:(g + 1) * ts, h * MLA_ROPE:(h + 1) * MLA_ROPE]
            knp[g, 0:ts, :] = kn_ref[g * ts:(g + 1) * ts, :]
            krnp[g, 0:ts, :] = krn_ref[g * ts:(g + 1) * ts, :]
        m_ref[...] = jnp.full(m_ref.shape, NEG_INF, F32)
        l_ref[...] = jnp.zeros(l_ref.shape, F32)
        acc_ref[...] = jnp.zeros(acc_ref.shape, F32)

    def update(g, s, k):
        m_prev = m_ref[g]
        m_new = jnp.maximum(m_prev, jnp.max(s, axis=-1, keepdims=True))
        p = jnp.exp(s - m_new)
        alpha = jnp.exp(m_prev - m_new)
        l_ref[g] = alpha * l_ref[g] + jnp.sum(p, axis=-1, keepdims=True)
        acc_ref[g] = alpha * acc_ref[g] + _dot(p.astype(BF16), k)
        m_ref[g] = m_new

    for g in range(group):
        for u in range(pages):
            kcat[g, u * page:(u + 1) * page, :] = lat_refs[g * pages + u][...].astype(BF16)
            krcat_t[g, :, u * page:(u + 1) * page] = krt_refs[g * pages + u][...].astype(BF16)
    for g in range(group):
        k = kcat[g]
        s = _dot_nt(qs[g].astype(BF16), k) + _dot(qrs[g].astype(BF16), krcat_t[g])
        update(g, s, k)

    @pl.when(j == npg - 1)
    def _():
        npad = knp.shape[1]
        tok = lax.broadcasted_iota(jnp.int32, (nrow, npad), 0) % ts
        col = lax.broadcasted_iota(jnp.int32, (nrow, npad), 1)
        for g in range(group):
            k = knp[g].astype(BF16)
            s = _dot_nt(qs[g].astype(BF16), k) + _dot_nt(qrs[g].astype(BF16), krnp[g].astype(BF16))
            update(g, jnp.where(col <= tok, s, NEG_INF), k)
            o = acc_ref[g] / l_ref[g]
            for h in range(MLA_HEADS):
                oh = o[h * ts:(h + 1) * ts, :].astype(BF16)
                o_ref[g * ts:(g + 1) * ts, h * MLA_V:(h + 1) * MLA_V] = _dot(oh, wuv_ref[h])


def _attn_sample(ql, qr, lat, kr, cache_lat, cache_krt, page_table, layer, wuv, row0, nseq, ts, pages_pref=16,
                 group_pref=4):
    n_pages = page_table.shape[1]
    page = cache_lat.shape[2]
    pages = _tile(n_pages, pages_pref, 1)
    npg = n_pages // pages
    group = _tile(nseq, group_pref, 1)
    gt = group * ts
    assert row0 % gt == 0
    blk0 = row0 // gt
    nrow = MLA_HEADS * ts
    kern = functools.partial(_attn_sample_kernel, pages=pages, npg=npg, ts=ts, page=page, group=group)

    def page_map(g, u):
        return lambda b, j, pt: (layer, pt[b * group + g, j * pages + u], 0, 0)

    slots = [(g, u) for g in range(group) for u in range(pages)]
    in_specs = [pl.BlockSpec((gt, MLA_HEADS * MLA_KV_RANK), lambda b, j, pt: (blk0 + b, 0)),
                pl.BlockSpec((gt, MLA_HEADS * MLA_ROPE), lambda b, j, pt: (blk0 + b, 0)),
                pl.BlockSpec((gt, MLA_KV_RANK), lambda b, j, pt: (blk0 + b, 0)),
                pl.BlockSpec((gt, MLA_ROPE), lambda b, j, pt: (blk0 + b, 0)),
                pl.BlockSpec(wuv.shape, lambda b, j, pt: (0, 0, 0))]
    in_specs += [pl.BlockSpec((None, None, page, MLA_KV_RANK), page_map(g, u)) for g, u in slots]
    in_specs += [pl.BlockSpec((None, None, MLA_ROPE, page), page_map(g, u)) for g, u in slots]
    grid_spec = pltpu.PrefetchScalarGridSpec(
        num_scalar_prefetch=1,
        grid=(nseq // group, npg),
        in_specs=in_specs,
        out_specs=pl.BlockSpec((gt, MLA_HEADS * MLA_V), lambda b, j, pt: (b, 0)),
        scratch_shapes=[pltpu.VMEM((group, nrow, MLA_KV_RANK), F32),
                        pltpu.VMEM((group, nrow, MLA_ROPE), F32),
                        pltpu.VMEM((group, pages * page, MLA_KV_RANK), BF16),
                        pltpu.VMEM((group, MLA_ROPE, pages * page), BF16),
                        pltpu.VMEM((group, LANE, MLA_KV_RANK), F32),
                        pltpu.VMEM((group, LANE, MLA_ROPE), F32),
                        pltpu.VMEM((group, nrow, 1), F32),
                        pltpu.VMEM((group, nrow, 1), F32),
                        pltpu.VMEM((group, nrow, MLA_KV_RANK), F32)])
    n_in = len(slots)
    return pl.pallas_call(
        kern,
        out_shape=jax.ShapeDtypeStruct((nseq * ts, MLA_HEADS * MLA_V), F32),
        grid_spec=grid_spec,
        compiler_params=_params("arbitrary", "arbitrary"),
        name="attn_sample",
    )(page_table, ql, qr, lat, kr, wuv, *([cache_lat] * n_in), *([cache_krt] * n_in))


def _mamba_kernel(z_ref, xbc_ref, misc_ref, cbuf_ref, s0_ref, cw_ref, cbias_ref, dtb_ref, alog_ref, dpar_ref,
                  nw_ref, y_ref, sfin_ref, xp_ref, s_ref, ysc_ref, *, L, nc):
    c = pl.program_id(1)
    pad = 8

    @pl.when(c == 0)
    def _():
        xp_ref[pad - 3:pad, :] = cbuf_ref[...]
        s_ref[...] = s0_ref[...]

    x = xbc_ref[...]
    xp_ref[pad:pad + L, :] = x
    cw = cw_ref[...]
    y = (cbias_ref[...] + cw[3:4] * x + cw[2:3] * xp_ref[pad - 1:pad - 1 + L, :]
         + cw[1:2] * xp_ref[pad - 2:pad - 2 + L, :] + cw[0:1] * xp_ref[pad - 3:pad - 3 + L, :])
    tail = xp_ref[pad + L - 3:pad + L, :]
    xp_ref[pad - 3:pad, :] = tail
    xa = _silu(y)
    bm = xa[:, SSM_INNER:SSM_INNER + SSM_GROUPS * SSM_STATE]
    cm = xa[:, SSM_INNER + SSM_GROUPS * SSM_STATE:]
    dt = jax.nn.softplus(misc_ref[:, LANE:2 * LANE] + dtb_ref[...])
    da = dt * (-jnp.exp(alog_ref[...]))
    ri = lax.broadcasted_iota(jnp.int32, (L, L), 0)
    ci = lax.broadcasted_iota(jnp.int32, (L, L), 1)
    causal = ri >= ci
    acs = _dot(causal.astype(F32), da, HI)
    acs_t = acs.T
    eacs = jnp.exp(acs)
    last = acs[L - 1:L, :]
    dte = jnp.exp(last - acs)
    elast = jnp.exp(last)
    dpar = dpar_ref[...]
    hg = SSM_HEADS // SSM_GROUPS
    for g in range(SSM_GROUPS):
        bg = bm[:, g * SSM_STATE:(g + 1) * SSM_STATE]
        cg = cm[:, g * SSM_STATE:(g + 1) * SSM_STATE]
        cbm = _dot_nt(cg, bg, HI)
        for hh in range(hg):
            h = g * hg + hh
            seg = jnp.where(causal, jnp.exp(acs[:, h:h + 1] - acs_t[h:h + 1, :]), 0.0)
            xh = xa[:, h * SSM_HEAD_DIM:(h + 1) * SSM_HEAD_DIM]
            xdt = xh * dt[:, h:h + 1]
            s_old = s_ref[h]
            yh = (_dot(cbm * seg, xdt, HI) + eacs[:, h:h + 1] * _dot_nt(cg, s_old, HI)
                  + dpar[:, h:h + 1] * xh)
            ysc_ref[:, h * SSM_HEAD_DIM:(h + 1) * SSM_HEAD_DIM] = yh
            s_ref[h] = elast[:, h:h + 1] * s_old + _dot_tn(xdt * dte[:, h:h + 1], bg, HI)
    yv = ysc_ref[...] * _silu(z_ref[...])
    gw = SSM_INNER // SSM_GROUPS
    nw = nw_ref[...]
    for g in range(SSM_GROUPS):
        yg = yv[:, g * gw:(g + 1) * gw]
        y_ref[:, g * gw:(g + 1) * gw] = _rms(yg, nw[:, g * gw:(g + 1) * gw])

    @pl.when(c == nc - 1)
    def _():
        sfin_ref[...] = s_ref[...]


def _mamba(p, cbuf, s0, cw, cbias, dtb, alog, dpar, nw, row0, nseq, t):
    L = min(SSM_CHUNK, t)
    nc = t // L
    blk0 = row0 // L
    kern = functools.partial(_mamba_kernel, L=L, nc=nc)
    vec = lambda w: pl.BlockSpec((1, w), lambda b, c: (0, 0))
    return pl.pallas_call(
        kern,
        out_shape=[jax.ShapeDtypeStruct((nseq * t, SSM_INNER), F32),
                   jax.ShapeDtypeStruct((nseq, SSM_HEADS, SSM_HEAD_DIM, SSM_STATE), F32)],
        grid=(nseq, nc),
        in_specs=[pl.BlockSpec((L, SSM_INNER), lambda b, c: (blk0 + b * nc + c, COL_Z // SSM_INNER)),
                  pl.BlockSpec((L, SSM_CONV_DIM), lambda b, c: (blk0 + b * nc + c, COL_XBC // SSM_CONV_DIM)),
                  pl.BlockSpec((L, MISC_W), lambda b, c: (blk0 + b * nc + c, COL_MISC // MISC_W)),
                  pl.BlockSpec((None, SSM_CONV - 1, SSM_CONV_DIM), lambda b, c: (b, 0, 0)),
                  pl.BlockSpec((None, SSM_HEADS, SSM_HEAD_DIM, SSM_STATE), lambda b, c: (b, 0, 0, 0)),
                  pl.BlockSpec((SSM_CONV, SSM_CONV_DIM), lambda b, c: (0, 0)),
                  vec(SSM_CONV_DIM), vec(LANE), vec(LANE), vec(LANE), vec(SSM_INNER)],
        out_specs=[pl.BlockSpec((L, SSM_INNER), lambda b, c: (b * nc + c, 0)),
                   pl.BlockSpec((None, SSM_HEADS, SSM_HEAD_DIM, SSM_STATE), lambda b, c: (b, 0, 0, 0))],
        scratch_shapes=[pltpu.VMEM((L + 8, SSM_CONV_DIM), F32),
                        pltpu.VMEM((SSM_HEADS, SSM_HEAD_DIM, SSM_STATE), F32),
                        pltpu.VMEM((L, SSM_INNER), F32)],
        compiler_params=_params("arbitrary", "arbitrary"),
        name="ssd_scan",
    )(p, p, p, cbuf, s0, cw, cbias, dtb, alog, dpar, nw)


def _gla_kernel(gq_ref, gk_ref, gv_ref, og_ref, misc_ref, s0_ref, wg_ref, bg_ref, gn_ref, o_ref, sfin_ref,
                st_ref, gl_ref, *, L, ls, nc):
    c = pl.program_id(1)

    @pl.when(c == 0)
    def _():
        for h in range(GLA_HEADS):
            st_ref[h] = s0_ref[h].T

    gl_ref[...] = jax.nn.log_sigmoid(_dot(misc_ref[:, LANE:2 * LANE], wg_ref[...], HI) + bg_ref[...]) \
        * (1.0 / GLA_GATE_NORMALIZER)
    ri = lax.broadcasted_iota(jnp.int32, (ls, ls), 0)
    ci = lax.broadcasted_iota(jnp.int32, (ls, ls), 1)
    tril = (ri >= ci).astype(F32)
    rows = lax.broadcasted_iota(jnp.int32, (ls, GLA_DK), 0)
    gn = gn_ref[...]

    def sub_chunk(sidx, carry):
        r0 = pl.multiple_of(sidx * ls, ls)
        for h in range(GLA_HEADS):
            q = gq_ref[pl.ds(r0, ls), h * GLA_DK:(h + 1) * GLA_DK] * (GLA_DK ** -0.5)
            k = gk_ref[pl.ds(r0, ls), h * GLA_DK:(h + 1) * GLA_DK]
            v = gv_ref[pl.ds(r0, ls), h * GLA_DV:(h + 1) * GLA_DV]
            g = gl_ref[pl.ds(r0, ls), h * GLA_DK:(h + 1) * GLA_DK]
            bcs = _dot(tril, g, HI)
            st = st_ref[h]
            o = _dot_nt((q * jnp.exp(bcs)).astype(BF16), st.astype(BF16))
            for j in range(ls):
                dj = jnp.where(rows >= j, bcs - bcs[j:j + 1, :], NEG_INF)
                w = jnp.sum(q * k[j:j + 1, :] * jnp.exp(dj), axis=-1, keepdims=True)
                o = o + w * v[j:j + 1, :]
            lastb = bcs[ls - 1:ls, :]
            st_ref[h] = st * jnp.exp(lastb) + _dot_tn(v.astype(BF16), (k * jnp.exp(lastb - bcs)).astype(BF16))
            o = o * lax.rsqrt(jnp.mean(o * o, axis=-1, keepdims=True) + EPS) * gn
            o = o * _silu(og_ref[pl.ds(r0, ls), h * GLA_DV:(h + 1) * GLA_DV])
            o_ref[pl.ds(r0, ls), h * GLA_DV:(h + 1) * GLA_DV] = o
        return carry

    lax.fori_loop(0, L // ls, sub_chunk, 0)

    @pl.when(c == nc - 1)
    def _():
        for h in range(GLA_HEADS):
            sfin_ref[h] = st_ref[h].T


def _gla(p, s0, wg, bg, gn, row0, nseq, t):
    L = min(128, t)
    ls = min(GLA_SUB, L)
    nc = t // L
    blk0 = row0 // L
    kern = functools.partial(_gla_kernel, L=L, ls=ls, nc=nc)
    rowblk = lambda w, col: pl.BlockSpec((L, w), lambda b, c: (blk0 + b * nc + c, col // w))
    return pl.pallas_call(
        kern,
        out_shape=[jax.ShapeDtypeStruct((nseq * t, GLA_VALUE), F32),
                   jax.ShapeDtypeStruct((nseq, GLA_HEADS, GLA_DK, GLA_DV), F32)],
        grid=(nseq, nc),
        in_specs=[rowblk(GLA_KEY, COL_GQ), rowblk(GLA_KEY, COL_GK), rowblk(GLA_VALUE, COL_GV),
                  rowblk(GLA_VALUE, COL_OG), rowblk(MISC_W, COL_MISC),
                  pl.BlockSpec((None, GLA_HEADS, GLA_DK, GLA_DV), lambda b, c: (b, 0, 0, 0)),
                  pl.BlockSpec((LANE, GLA_KEY), lambda b, c: (0, 0)),
                  pl.BlockSpec((1, GLA_KEY), lambda b, c: (0, 0)),
                  pl.BlockSpec((1, GLA_DV), lambda b, c: (0, 0))],
        out_specs=[pl.BlockSpec((L, GLA_VALUE), lambda b, c: (b * nc + c, 0)),
                   pl.BlockSpec((None, GLA_HEADS, GLA_DK, GLA_DV), lambda b, c: (b, 0, 0, 0))],
        scratch_shapes=[pltpu.VMEM((GLA_HEADS, GLA_DV, GLA_DK), F32),
                        pltpu.VMEM((L, GLA_KEY), F32)],
        compiler_params=_params("arbitrary", "arbitrary"),
        name="gla_scan",
    )(p, p, p, p, p, s0, wg, bg, gn)


def _merge_kernel(oa_ref, ob_ref, oc_ref, ga_ref, gb_ref, gc_ref, wb_ref, o_ref):
    acc = jax.nn.sigmoid(ga_ref[...]) * _dot(oa_ref[...].astype(BF16), wb_ref[0])
    acc += jax.nn.sigmoid(gb_ref[...]) * _dot(ob_ref[...].astype(BF16), wb_ref[1])
    acc += jax.nn.sigmoid(gc_ref[...]) * _dot(oc_ref[...].astype(BF16), wb_ref[2])
    o_ref[...] = acc.astype(BF16)


def _merge(oa, ob, oc, p, wb, layer, tm_pref=512, tn=512):
    rows = oa.shape[0]
    tm = _tile(rows, tm_pref, 16)
    nj = D_MODEL // tn
    br = lambda: pl.BlockSpec((tm, BRANCH_W), lambda j, i: (i, 0))
    gate = lambda n: pl.BlockSpec((tm, tn), lambda j, i: (i, (COL_GATES + n * D_MODEL) // tn + j))
    return pl.pallas_call(
        _merge_kernel,
        out_shape=jax.ShapeDtypeStruct((rows, D_MODEL), BF16),
        grid=(nj, rows // tm),
        in_specs=[br(), br(), br(), gate(0), gate(1), gate(2),
                  pl.BlockSpec((None, 3, BRANCH_W, tn), lambda j, i: (layer, 0, 0, j))],
        out_specs=pl.BlockSpec((tm, tn), lambda j, i: (i, j)),
        compiler_params=_params("arbitrary", "arbitrary"),
        name="branch_merge",
    )(oa, ob, oc, p, p, p, wb)


def _out_proj_kernel(m_ref, w_ref, x_ref, o_ref):
    o_ref[...] = x_ref[...] + _dot(m_ref[...], w_ref[...])


def _out_proj(m, w, x, layer, tm_pref=512, tn=512):
    rows, k = m.shape
    n = w.shape[2]
    tm = _tile(rows, tm_pref, 16)
    return pl.pallas_call(
        _out_proj_kernel,
        out_shape=jax.ShapeDtypeStruct((rows, n), F32),
        grid=(n // tn, rows // tm),
        in_specs=[pl.BlockSpec((tm, k), lambda j, i: (i, 0)),
                  pl.BlockSpec((None, k, tn), lambda j, i: (layer, 0, j)),
                  pl.BlockSpec((tm, tn), lambda j, i: (i, j))],
        out_specs=pl.BlockSpec((tm, tn), lambda j, i: (i, j)),
        compiler_params=_params("arbitrary", "arbitrary"),
        name="out_proj",
    )(m, w, x)


def _top16(s):
    n = s.shape[0]
    iota = lax.broadcasted_iota(jnp.int32, s.shape, 0)
    rank = jnp.full(s.shape, PEER_TOPK, jnp.int32)
    vals = []
    for r in range(PEER_TOPK):
        m = jnp.max(s, axis=0, keepdims=True)
        pos = jnp.min(jnp.where(s == m, iota, n), axis=0, keepdims=True)
        hit = iota == pos
        rank = jnp.where(hit, r, rank)
        s = jnp.where(hit, NEG_INF, s)
        vals.append(m)
    return rank, vals


_CAND_ROWS = [PEER_TOPK // (r1 + 1) for r1 in range(PEER_TOPK)]
_CAND_N = sum(_CAND_ROWS)
_CAND_PAD = -(-_CAND_N // 8) * 8


def _peer_route_kernel(x_ref, g_ref, wqt_ref, sk_ref, hq_ref, rank2_ref, cnt_ref, a_ref, b_ref, cand_ref):
    hq = _rms(x_ref[...], g_ref[...]).astype(BF16)
    hq_ref[...] = hq
    qt = _dot_nt(wqt_ref[...], hq)
    t = qt.shape[1]
    iota_c = lax.broadcasted_iota(jnp.int32, (_CAND_PAD, t), 0)
    cand_ref[_CAND_N:_CAND_PAD, :] = jnp.full((_CAND_PAD - _CAND_N, t), NEG_INF, F32)
    for h in range(PEER_HEADS):
        base = h * 2 * PEER_HALF
        s1 = _dot(sk_ref[0, h], qt[base:base + PEER_HALF].astype(BF16))
        s2 = _dot(sk_ref[1, h], qt[base + PEER_HALF:base + 2 * PEER_HALF].astype(BF16))
        rank1, v1 = _top16(s1)
        rank2, v2 = _top16(s2)
        v2s = jnp.concatenate(v2, axis=0)
        off = 0
        for r1, n in enumerate(_CAND_ROWS):
            cand_ref[off:off + n, :] = v1[r1] + v2s[0:n]
            off += n
        cand = cand_ref[...]
        top = v1[0] + v2[0]
        picked = jnp.zeros((_CAND_PAD, t), F32)
        z = jnp.zeros((1, t), F32)
        for _ in range(PEER_TOPK):
            m = jnp.max(cand, axis=0, keepdims=True)
            pos = jnp.min(jnp.where(cand == m, iota_c, _CAND_PAD), axis=0, keepdims=True)
            hit = iota_c == pos
            z = z + jnp.exp(m - top)
            picked = jnp.where(hit, 1.0, picked)
            cand = jnp.where(hit, NEG_INF, cand)
        cntrow = jnp.zeros(s1.shape, F32)
        off = 0
        for r1, n in enumerate(_CAND_ROWS):
            cnt = jnp.sum(picked[off:off + n], axis=0, keepdims=True)
            cntrow = jnp.where(rank1 == r1, cnt, cntrow)
            off += n
        rank2_ref[h] = rank2.astype(F32).astype(BF16)
        cnt_ref[h] = cntrow
        a_ref[h] = jnp.where(rank1 < PEER_TOPK, jnp.exp(s1 - v1[0]), 0.0) / z
        b_ref[h] = jnp.where(rank2 < PEER_TOPK, jnp.exp(s2 - v2[0]), 0.0).astype(BF16)


def _peer_route(x, g, wqt, sk, layer, tt):
    rows, d = x.shape
    tab = jax.ShapeDtypeStruct((PEER_HEADS, PEER_NKEYS, rows), F32)
    tab16 = jax.ShapeDtypeStruct((PEER_HEADS, PEER_NKEYS, rows), BF16)
    tab_spec = pl.BlockSpec((PEER_HEADS, PEER_NKEYS, tt), lambda i: (0, 0, i))
    return pl.pallas_call(
        _peer_route_kernel,
        out_shape=[jax.ShapeDtypeStruct((rows, d), BF16), tab16, tab, tab, tab16],
        grid=(rows // tt,),
        in_specs=[pl.BlockSpec((tt, d), lambda i: (i, 0)),
                  pl.BlockSpec((1, d), lambda i: (0, 0)),
                  pl.BlockSpec((None,) + wqt.shape[1:], lambda i: (layer, 0, 0)),
                  pl.BlockSpec(sk.shape, lambda i: (0, 0, 0, 0))],
        out_specs=[pl.BlockSpec((tt, d), lambda i: (i, 0)), tab_spec, tab_spec, tab_spec, tab_spec],
        scratch_shapes=[pltpu.VMEM((_CAND_PAD, tt), F32)],
        compiler_params=_params("arbitrary"),
        name="peer_route",
    )(x, g, wqt, sk)


def _peer_dense_kernel(hq_ref, x_ref, u_ref, v_ref, rank2_ref, cnt_ref, a_ref, b_ref, o_ref, acc_ref, wa_ref,
                       *, eb, ne, half):
    e = pl.program_id(1)

    @pl.when(e == 0)
    def _():
        acc_ref[...] = jnp.zeros(acc_ref.shape, F32)

    nsub = half // PEER_NKEYS
    for hf in range(eb // half):
        act = _dot_nt(u_ref[hf * half:(hf + 1) * half, :], hq_ref[...])
        act = 0.5 * act * (1.0 + lax.erf(act * (2.0 ** -0.5)))
        for ii in range(nsub):
            i = hf * nsub + ii
            w = None
            for h in range(PEER_HEADS):
                c = cnt_ref[h, i:i + 1, :].astype(BF16)
                av = a_ref[h, i:i + 1, :].astype(BF16)
                term = av * jnp.where(rank2_ref[h] < c, b_ref[h], jnp.zeros((), BF16))
                w = term if w is None else w + term
            r0 = hf * half + ii * PEER_NKEYS
            wa_ref[r0:r0 + PEER_NKEYS, :] = w * act[ii * PEER_NKEYS:(ii + 1) * PEER_NKEYS].astype(BF16)
    acc_ref[...] += _dot_tn(wa_ref[...], v_ref[...])

    @pl.when(e == ne - 1)
    def _():
        o_ref[...] = x_ref[...] + acc_ref[...]


def _peer_dense(hq, x, u, v, rank2, cntrow, a, b, layer, tt, eb=1024, half=256):
    rows, d = x.shape
    n_exp = u.shape[1]
    ne = n_exp // eb
    kern = functools.partial(_peer_dense_kernel, eb=eb, ne=ne, half=half)
    tab_spec = pl.BlockSpec((PEER_HEADS, PEER_NKEYS, tt), lambda i, e: (0, 0, i))
    row_spec = pl.BlockSpec((PEER_HEADS, eb // PEER_NKEYS, tt), lambda i, e: (0, e, i))
    return pl.pallas_call(
        kern,
        out_shape=jax.ShapeDtypeStruct((rows, d), F32),
        grid=(rows // tt, ne),
        in_specs=[pl.BlockSpec((tt, d), lambda i, e: (i, 0)),
                  pl.BlockSpec((tt, d), lambda i, e: (i, 0)),
                  pl.BlockSpec((None, eb, d), lambda i, e: (layer, e, 0)),
                  pl.BlockSpec((None, eb, d), lambda i, e: (layer, e, 0)),
                  tab_spec, row_spec, row_spec, tab_spec],
        out_specs=pl.BlockSpec((tt, d), lambda i, e: (i, 0)),
        scratch_shapes=[pltpu.VMEM((tt, d), F32), pltpu.VMEM((eb, tt), BF16)],
        compiler_params=_params("arbitrary", "arbitrary"),
        name="peer_dense",
    )(hq, x, u, v, rank2, cntrow, a, b)


def _final_norm_kernel(x_ref, g_ref, o_ref):
    o_ref[...] = _rms(x_ref[...], g_ref[...])


def _final_norm(x, g, tm_pref=512):
    rows, d = x.shape
    tm = _tile(rows, tm_pref)
    return pl.pallas_call(
        _final_norm_kernel,
        out_shape=jax.ShapeDtypeStruct((rows, d), F32),
        grid=(rows // tm,),
        in_specs=[pl.BlockSpec((tm, d), lambda i: (i, 0)), pl.BlockSpec((1, d), lambda i: (0, 0))],
        out_specs=pl.BlockSpec((tm, d), lambda i: (i, 0)),
        compiler_params=_params("arbitrary"),
        name="final_norm",
    )(x, g)


def _pad_lanes(v, width=LANE):
    return jnp.pad(v, [(0, 0)] * (v.ndim - 1) + [(0, width - v.shape[-1])])


def _rot_half(w):
    half = w.shape[-1] // 2
    return jnp.concatenate([w[..., half:], w[..., :half]], axis=-1)


def _layout_w_in(w_in):
    widths = (MLA_Q_RANK, MLA_KV_RANK, MLA_ROPE, SSM_INNER, SSM_CONV_DIM, SSM_HEADS,
              GLA_KEY, GLA_KEY, GLA_VALUE, GLA_GATE_RANK, GLA_VALUE, 3 * D_MODEL)
    offs = [0]
    for w in widths:
        offs.append(offs[-1] + w)
    cq, ckv, kr, z, xbc, dt, gq, gk, gv, glr, og, gates = [w_in[..., offs[i]:offs[i + 1]] for i in range(len(widths))]
    zero = jnp.zeros(w_in.shape[:-1] + (MISC_W - 2 * MLA_ROPE - SSM_HEADS - GLA_GATE_RANK,), w_in.dtype)
    out = jnp.concatenate([gates, z, og, gv, xbc, gq, gk, cq, ckv, kr, _rot_half(kr), dt, glr, zero], axis=-1)
    return out.astype(BF16)


def _rope_tables(pos):
    half = MLA_ROPE // 2
    inv = ROPE_BASE ** (-jnp.arange(half, dtype=F32) / half)
    ang = pos.astype(F32)[:, None] * inv[None, :]
    cos, sin = jnp.cos(ang), jnp.sin(ang)
    cos_t = jnp.tile(jnp.concatenate([cos, cos], axis=-1), (1, MLA_HEADS))
    sin_t = jnp.tile(jnp.concatenate([-sin, sin], axis=-1), (1, MLA_HEADS))
    return cos_t, sin_t


def kernel(x_prompt, x_sample, cache_mla_latent, cache_mla_krope, page_table, state_ssm_conv, state_ssm, state_gla, norm_mix, w_in, mla_q_norm, mla_w_uq, mla_kv_norm, mla_w_uk, mla_w_uv, ssm_conv_w, ssm_conv_b, ssm_dt_bias, ssm_a_log, ssm_d, ssm_norm, gla_gate_w, gla_gate_b, gla_norm, w_branch, w_out, norm_ffn, peer_wq, peer_subkeys, peer_u, peer_v, norm_final):
    bp, tp, d = x_prompt.shape
    bs, ts, _ = x_sample.shape
    depth = w_in.shape[0]
    n_pages = page_table.shape[1]
    past_len = n_pages * cache_mla_latent.shape[2]
    rows_p = bp * tp
    rows_s = bs * ts
    rows = rows_p + rows_s

    x = jnp.concatenate([x_prompt.reshape(rows_p, d), x_sample.reshape(rows_s, d)], axis=0)
    pos = jnp.concatenate([jnp.tile(jnp.arange(tp), bp), jnp.tile(past_len + jnp.arange(ts), bs)])
    cos_t, sin_t = _rope_tables(pos)
    w_in_l = _layout_w_in(w_in)
    uq_nope = mla_w_uq[..., :MLA_NOPE].reshape(depth, MLA_Q_RANK, MLA_HEADS * MLA_NOPE)
    uq_rope = mla_w_uq[..., MLA_NOPE:]
    wq_all = jnp.concatenate([uq_nope,
                              uq_rope.reshape(depth, MLA_Q_RANK, MLA_HEADS * MLA_ROPE),
                              _rot_half(uq_rope).reshape(depth, MLA_Q_RANK, MLA_HEADS * MLA_ROPE)],
                             axis=-1).astype(BF16)
    wuk_t = jnp.transpose(mla_w_uk, (0, 2, 3, 1)).astype(BF16)
    wuv = jnp.transpose(mla_w_uv, (0, 2, 1, 3)).astype(BF16)
    wuvt = jnp.transpose(mla_w_uv, (0, 2, 3, 1)).astype(BF16)
    dtb = _pad_lanes(ssm_dt_bias)
    alog = _pad_lanes(ssm_a_log)
    dpar = _pad_lanes(ssm_d)
    wg = jnp.zeros((depth, LANE, GLA_KEY), F32).at[:, SSM_HEADS:SSM_HEADS + GLA_GATE_RANK, :].set(gla_gate_w)
    wb = w_branch.astype(BF16)
    wo = w_out.astype(BF16)
    wqt = jnp.swapaxes(peer_wq, 1, 2).astype(BF16)
    sk = peer_subkeys.astype(BF16)
    ub = peer_u.astype(BF16)
    vb = peer_v.astype(BF16)
    cache_krt = jnp.swapaxes(cache_mla_krope, 2, 3)
    conv0 = jnp.zeros((bp, SSM_CONV - 1, SSM_CONV_DIM), F32)
    ssm0 = jnp.zeros((bp, SSM_HEADS, SSM_HEAD_DIM, SSM_STATE), F32)
    gla0 = jnp.zeros((bp, GLA_HEADS, GLA_DK, GLA_DV), F32)
    tt_route = _tile(rows, 256, LANE)
    tt_dense = _tile(rows, 512, LANE)

    new_p = [[], [], [], [], []]
    new_s = [[], [], [], [], []]
    for l in range(depth):
        p = _norm_matmul(x, norm_mix[l][None], w_in_l, l)
        ql, qr, lat, kr = _mla_prep(p, cos_t, sin_t, mla_q_norm[l][None], mla_kv_norm[l][None], wq_all[l], wuk_t[l])
        oa_p = _attn_prompt(ql, qr, lat, kr, wuvt[l], bp, tp)
        oa_s = _attn_sample(ql, qr, lat, kr, cache_mla_latent, cache_krt, page_table, l, wuv[l],
                            rows_p, bs, ts)
        ssm_args = (ssm_conv_w[l], ssm_conv_b[l][None], dtb[l][None], alog[l][None], dpar[l][None],
                    ssm_norm[l][None])
        ob_p, ssm_p = _mamba(p, conv0, ssm0, *ssm_args, 0, bp, tp)
        ob_s, ssm_s = _mamba(p, state_ssm_conv[l], state_ssm[l], *ssm_args, rows_p, bs, ts)
        gla_args = (wg[l], gla_gate_b[l][None], gla_norm[l][None])
        oc_p, gla_p = _gla(p, gla0, *gla_args, 0, bp, tp)
        oc_s, gla_s = _gla(p, state_gla[l], *gla_args, rows_p, bs, ts)
        merged = _merge(jnp.concatenate([oa_p, oa_s]), jnp.concatenate([ob_p, ob_s]),
                        jnp.concatenate([oc_p, oc_s]), p, wb, l)
        x = _out_proj(merged, wo, x, l)
        hq, rank2, cntrow, ga, gb = _peer_route(x, norm_ffn[l][None], wqt, sk[l], l, tt_route)
        x = _peer_dense(hq, x, ub, vb, rank2, cntrow, ga, gb, l, tt_dense)

        xbc = p[:, COL_XBC:COL_XBC + SSM_CONV_DIM]
        conv_p = xbc[:rows_p].reshape(bp, tp, SSM_CONV_DIM)[:, tp - (SSM_CONV - 1):]
        conv_s = xbc[rows_p:].reshape(bs, ts, SSM_CONV_DIM)[:, ts - (SSM_CONV - 1):]
        for lst, a in zip(new_p, (lat[:rows_p].reshape(bp, tp, MLA_KV_RANK), kr[:rows_p].reshape(bp, tp, MLA_ROPE),
                                  conv_p, ssm_p, gla_p)):
            lst.append(a)
        for lst, a in zip(new_s, (lat[rows_p:].reshape(bs, ts, MLA_KV_RANK), kr[rows_p:].reshape(bs, ts, MLA_ROPE),
                                  conv_s, ssm_s, gla_s)):
            lst.append(a)

    y = _final_norm(x, norm_final[None])
    y_prompt = y[:rows_p].reshape(bp, tp, d)
    y_sample = y[rows_p:].reshape(bs, ts, d)
    outs_p = [jnp.stack(a) for a in new_p]
    outs_s = [jnp.stack(a) for a in new_s]
    return (y_prompt, y_sample, *outs_p, *outs_s)
```

```python
import functools
import math

import jax
import jax.numpy as jnp
from jax import lax
from jax.experimental import pallas as pl
from jax.experimental.pallas import tpu as pltpu

F32 = jnp.float32
BF16 = jnp.bfloat16
HI = lax.Precision.HIGHEST
NEG_INF = float("-inf")

EPS = 1e-6
D_MODEL = 2048
BRANCH_W = 1024
MLA_HEADS = 8
MLA_NOPE = 128
MLA_ROPE = 64
MLA_V = 128
MLA_Q_RANK = 512
MLA_KV_RANK = 256
MLA_SCALE = (MLA_NOPE + MLA_ROPE) ** -0.5
ROPE_BASE = 10000.0
SSM_INNER = 1024
SSM_HEAD_DIM = 64
SSM_HEADS = 16
SSM_GROUPS = 2
SSM_STATE = 128
SSM_CONV = 4
SSM_CONV_DIM = SSM_INNER + 2 * SSM_GROUPS * SSM_STATE
SSM_CHUNK = 128
GLA_HEADS = 4
GLA_DK = 128
GLA_DV = 256
GLA_KEY = 512
GLA_VALUE = 1024
GLA_GATE_RANK = 16
GLA_GATE_NORMALIZER = 16.0
GLA_SUB = 16
PEER_HEADS = 8
PEER_NKEYS = 128
PEER_TOPK = 16
PEER_HALF = 128

LANE = 128
VMEM_LIMIT = 56 * 1024 * 1024

COL_GATES = 0
COL_Z = 6144
COL_OG = 7168
COL_GV = 8192
COL_XBC = 9216
COL_GQ = 10752
COL_GK = 11264
COL_CQ = 11776
COL_CKV = 12288
COL_MISC = 12544
IN_COLS = 12800
MISC_W = 256


def _tile(n, pref, mult=8):
    for t in range(min(n, pref), 0, -1):
        if n % t == 0 and t % mult == 0:
            return t
    return n


def _params(*sem):
    return pltpu.CompilerParams(dimension_semantics=sem, vmem_limit_bytes=VMEM_LIMIT)


def _rms(x, g):
    return x * lax.rsqrt(jnp.mean(x * x, axis=-1, keepdims=True) + EPS) * g


def _dot(a, b, prec=None):
    return jnp.dot(a, b, preferred_element_type=F32, precision=prec)


def _dot_nt(a, b, prec=None):
    return lax.dot_general(a, b, (((1,), (1,)), ((), ())), preferred_element_type=F32, precision=prec)


def _dot_tn(a, b, prec=None):
    return lax.dot_general(a, b, (((0,), (0,)), ((), ())), preferred_element_type=F32, precision=prec)


def _silu(x):
    return x * jax.nn.sigmoid(x)


def _norm_matmul_kernel(x_ref, g_ref, w_ref, o_ref, h_ref):
    @pl.when(pl.program_id(1) == 0)
    def _():
        h_ref[...] = _rms(x_ref[...], g_ref[...]).astype(BF16)

    o_ref[...] = _dot(h_ref[...], w_ref[...])


def _norm_matmul(x, g, w, layer, tm_pref=1024, tn=512):
    m, k = x.shape
    n = w.shape[2]
    tm = _tile(m, tm_pref)
    return pl.pallas_call(
        _norm_matmul_kernel,
        out_shape=jax.ShapeDtypeStruct((m, n), F32),
        grid=(m // tm, n // tn),
        in_specs=[pl.BlockSpec((tm, k), lambda i, j: (i, 0)),
                  pl.BlockSpec((1, k), lambda i, j: (0, 0)),
                  pl.BlockSpec((None, k, tn), lambda i, j: (layer, 0, j))],
        out_specs=pl.BlockSpec((tm, tn), lambda i, j: (i, j)),
        scratch_shapes=[pltpu.VMEM((tm, k), BF16)],
        compiler_params=_params("arbitrary", "arbitrary"),
        name="norm_in_proj",
    )(x, g, w)


def _mla_prep_kernel(cq_ref, ckv_ref, misc_ref, cos_ref, sin_ref, qn_ref, kvn_ref, wq_ref, wuk_ref,
                     ql_ref, qr_ref, lat_ref, kr_ref):
    cqn = _rms(cq_ref[...], qn_ref[...]).astype(BF16)
    q = _dot(cqn, wq_ref[...])
    cos = cos_ref[...]
    sin = sin_ref[...]
    nope_w = MLA_HEADS * MLA_NOPE
    rope_w = MLA_HEADS * MLA_ROPE
    qr_ref[...] = (q[:, nope_w:nope_w + rope_w] * cos + q[:, nope_w + rope_w:] * sin) * MLA_SCALE
    for h in range(MLA_HEADS):
        qh = q[:, h * MLA_NOPE:(h + 1) * MLA_NOPE].astype(BF16)
        ql_ref[:, h * MLA_KV_RANK:(h + 1) * MLA_KV_RANK] = _dot(qh, wuk_ref[h]) * MLA_SCALE
    lat_ref[...] = _rms(ckv_ref[...], kvn_ref[...])
    misc = misc_ref[...]
    kr_ref[...] = misc[:, 0:MLA_ROPE] * cos[:, 0:MLA_ROPE] + misc[:, MLA_ROPE:2 * MLA_ROPE] * sin[:, 0:MLA_ROPE]


def _mla_prep(p, cos, sin, qn, kvn, wq, wuk, tm_pref=256):
    rows = p.shape[0]
    tm = _tile(rows, tm_pref)
    lat_w = MLA_HEADS * MLA_KV_RANK
    rope_w = MLA_HEADS * MLA_ROPE
    return pl.pallas_call(
        _mla_prep_kernel,
        out_shape=[jax.ShapeDtypeStruct((rows, lat_w), F32),
                   jax.ShapeDtypeStruct((rows, rope_w), F32),
                   jax.ShapeDtypeStruct((rows, MLA_KV_RANK), F32),
                   jax.ShapeDtypeStruct((rows, MLA_ROPE), F32)],
        grid=(rows // tm,),
        in_specs=[pl.BlockSpec((tm, MLA_Q_RANK), lambda i: (i, COL_CQ // MLA_Q_RANK)),
                  pl.BlockSpec((tm, MLA_KV_RANK), lambda i: (i, COL_CKV // MLA_KV_RANK)),
                  pl.BlockSpec((tm, MISC_W), lambda i: (i, COL_MISC // MISC_W)),
                  pl.BlockSpec((tm, rope_w), lambda i: (i, 0)),
                  pl.BlockSpec((tm, rope_w), lambda i: (i, 0)),
                  pl.BlockSpec((1, MLA_Q_RANK), lambda i: (0, 0)),
                  pl.BlockSpec((1, MLA_KV_RANK), lambda i: (0, 0)),
                  pl.BlockSpec(wq.shape, lambda i: (0, 0)),
                  pl.BlockSpec(wuk.shape, lambda i: (0, 0, 0))],
        out_specs=[pl.BlockSpec((tm, lat_w), lambda i: (i, 0)),
                   pl.BlockSpec((tm, rope_w), lambda i: (i, 0)),
                   pl.BlockSpec((tm, MLA_KV_RANK), lambda i: (i, 0)),
                   pl.BlockSpec((tm, MLA_ROPE), lambda i: (i, 0))],
        compiler_params=_params("arbitrary"),
        name="mla_prep",
    )(p, p, p, cos, sin, qn, kvn, wq, wuk)


def _attn_prompt_kernel(ql_ref, qr_ref, k_ref, kr_ref, wuvt_ref, o_ref, qs, qrs, m_ref, l_ref, acc_ref, *, tq, tk):
    qi = pl.program_id(1)
    ki = pl.program_id(2)
    ncol = MLA_HEADS * tq

    @pl.when(ki == 0)
    def _():
        for h in range(MLA_HEADS):
            qs[h * tq:(h + 1) * tq, :] = ql_ref[:, h * MLA_KV_RANK:(h + 1) * MLA_KV_RANK].astype(BF16)
            qrs[h * tq:(h + 1) * tq, :] = qr_ref[:, h * MLA_ROPE:(h + 1) * MLA_ROPE].astype(BF16)
        m_ref[...] = jnp.full(m_ref.shape, NEG_INF, F32)
        l_ref[...] = jnp.zeros(l_ref.shape, F32)
        acc_ref[...] = jnp.zeros(acc_ref.shape, F32)

    @pl.when(ki <= qi)
    def _():
        k = k_ref[...].astype(BF16)
        kr = kr_ref[...].astype(BF16)
        st = _dot_nt(k, qs[...]) + _dot_nt(kr, qrs[...])
        kpos = ki * tk + lax.broadcasted_iota(jnp.int32, (tk, ncol), 0)
        qpos = qi * tq + (lax.broadcasted_iota(jnp.int32, (tk, ncol), 1) & (tq - 1))
        st = jnp.where(kpos <= qpos, st, NEG_INF)
        m_prev = m_ref[...]
        m_new = jnp.maximum(m_prev, jnp.max(st, axis=0, keepdims=True))
        p = jnp.exp(st - m_new)
        alpha = jnp.exp(m_prev - m_new)
        l_ref[...] = alpha * l_ref[...] + jnp.sum(p, axis=0, keepdims=True)
        acc_ref[...] = alpha * acc_ref[...] + _dot_tn(k, p.astype(BF16))
        m_ref[...] = m_new

    @pl.when(ki == qi)
    def _():
        o = (acc_ref[...] / l_ref[...]).astype(BF16)
        for h in range(MLA_HEADS):
            o_ref[:, h * MLA_V:(h + 1) * MLA_V] = _dot(wuvt_ref[h], o[:, h * tq:(h + 1) * tq]).T


def _attn_prompt(ql, qr, lat, kr, wuvt, nb, t, tq_pref=256):
    tq = _tile(t, tq_pref)
    assert tq & (tq - 1) == 0
    nq = t // tq
    ncol = MLA_HEADS * tq
    kern = functools.partial(_attn_prompt_kernel, tq=tq, tk=tq)
    return pl.pallas_call(
        kern,
        out_shape=jax.ShapeDtypeStruct((nb * t, MLA_HEADS * MLA_V), F32),
        grid=(nb, nq, nq),
        in_specs=[pl.BlockSpec((tq, MLA_HEADS * MLA_KV_RANK), lambda b, qi, ki: (b * nq + qi, 0)),
                  pl.BlockSpec((tq, MLA_HEADS * MLA_ROPE), lambda b, qi, ki: (b * nq + qi, 0)),
                  pl.BlockSpec((tq, MLA_KV_RANK), lambda b, qi, ki: (b * nq + jnp.minimum(ki, qi), 0)),
                  pl.BlockSpec((tq, MLA_ROPE), lambda b, qi, ki: (b * nq + jnp.minimum(ki, qi), 0)),
                  pl.BlockSpec(wuvt.shape, lambda b, qi, ki: (0, 0, 0))],
        out_specs=pl.BlockSpec((tq, MLA_HEADS * MLA_V), lambda b, qi, ki: (b * nq + qi, 0)),
        scratch_shapes=[pltpu.VMEM((ncol, MLA_KV_RANK), BF16),
                        pltpu.VMEM((ncol, MLA_ROPE), BF16),
                        pltpu.VMEM((1, ncol), F32),
                        pltpu.VMEM((1, ncol), F32),
                        pltpu.VMEM((MLA_KV_RANK, ncol), F32)],
        compiler_params=_params("arbitrary", "arbitrary", "arbitrary"),
        name="attn_prompt",
    )(ql, qr, lat, kr, wuvt)


def _attn_sample_kernel(pt_ref, ql_ref, qr_ref, kn_ref, krn_ref, wuv_ref, *rest, pages, npg, ts, page, group):
    n_in = group * pages
    lat_refs = rest[:n_in]
    krt_refs = rest[n_in:2 * n_in]
    o_ref = rest[2 * n_in]
    qs, qrs, kcat, krcat_t, knp, krnp, m_ref, l_ref, acc_ref = rest[2 * n_in + 1:]
    j = pl.program_id(1)
    nrow = MLA_HEADS * ts

    @pl.when(j == 0)
    def _():
        knp[...] = jnp.zeros(knp.shape, F32)
        krnp[...] = jnp.zeros(krnp.shape, F32)
        for g in range(group):
            for h in range(MLA_HEADS):
                qs[g, h * ts:(h + 1) * ts, :] = ql_ref[g * ts:(g + 1) * ts, h * MLA_KV_RANK:(h + 1) * MLA_KV_RANK]
                qrs[g, h * ts:(h + 1) * ts, :] = qr_ref[g * ts:(g + 1) * ts, h * MLA_ROPE:(h + 1) * MLA_ROPE]
            knp[g, 0:ts, :] = kn_ref[g * ts:(g + 1) * ts, :]
            krnp[g, 0:ts, :] = krn_ref[g * ts:(g + 1) * ts, :]
        m_ref[...] = jnp.full(m_ref.shape, NEG_INF, F32)
        l_ref[...] = jnp.zeros(l_ref.shape, F32)
        acc_ref[...] = jnp.zeros(acc_ref.shape, F32)

    def update(g, s, k):
        m_prev = m_ref[g]
        m_new = jnp.maximum(m_prev, jnp.max(s, axis=-1, keepdims=True))
        p = jnp.exp(s - m_new)
        alpha = jnp.exp(m_prev - m_new)
        l_ref[g] = alpha * l_ref[g] + jnp.sum(p, axis=-1, keepdims=True)
        acc_ref[g] = alpha * acc_ref[g] + _dot(p.astype(BF16), k)
        m_ref[g] = m_new

    for g in range(group):
        for u in range(pages):
            kcat[g, u * page:(u + 1) * page, :] = lat_refs[g * pages + u][...].astype(BF16)
            krcat_t[g, :, u * page:(u + 1) * page] = krt_refs[g * pages + u][...].astype(BF16)
    for g in range(group):
        k = kcat[g]
        s = _dot_nt(qs[g].astype(BF16), k) + _dot(qrs[g].astype(BF16), krcat_t[g])
        update(g, s, k)

    @pl.when(j == npg - 1)
    def _():
        npad = knp.shape[1]
        tok = lax.broadcasted_iota(jnp.int32, (nrow, npad), 0) % ts
        col = lax.broadcasted_iota(jnp.int32, (nrow, npad), 1)
        for g in range(group):
            k = knp[g].astype(BF16)
            s = _dot_nt(qs[g].astype(BF16), k) + _dot_nt(qrs[g].astype(BF16), krnp[g].astype(BF16))
            update(g, jnp.where(col <= tok, s, NEG_INF), k)
            o = acc_ref[g] / l_ref[g]
            for h in range(MLA_HEADS):
                oh = o[h * ts:(h + 1) * ts, :].astype(BF16)
                o_ref[g * ts:(g + 1) * ts, h * MLA_V:(h + 1) * MLA_V] = _dot(oh, wuv_ref[h])


def _attn_sample(ql, qr, lat, kr, cache_lat, cache_krt, page_table, layer, wuv, row0, nseq, ts, pages_pref=16,
                 group_pref=4):
    n_pages = page_table.shape[1]
    page = cache_lat.shape[2]
    pages = _tile(n_pages, pages_pref, 1)
    npg = n_pages // pages
    group = _tile(nseq, group_pref, 1)
    gt = group * ts
    assert row0 % gt == 0
    blk0 = row0 // gt
    nrow = MLA_HEADS * ts
    kern = functools.partial(_attn_sample_kernel, pages=pages, npg=npg, ts=ts, page=page, group=group)

    def page_map(g, u):
        return lambda b, j, pt: (layer, pt[b * group + g, j * pages + u], 0, 0)

    slots = [(g, u) for g in range(group) for u in range(pages)]
    in_specs = [pl.BlockSpec((gt, MLA_HEADS * MLA_KV_RANK), lambda b, j, pt: (blk0 + b, 0)),
                pl.BlockSpec((gt, MLA_HEADS * MLA_ROPE), lambda b, j, pt: (blk0 + b, 0)),
                pl.BlockSpec((gt, MLA_KV_RANK), lambda b, j, pt: (blk0 + b, 0)),
                pl.BlockSpec((gt, MLA_ROPE), lambda b, j, pt: (blk0 + b, 0)),
                pl.BlockSpec(wuv.shape, lambda b, j, pt: (0, 0, 0))]
    in_specs += [pl.BlockSpec((None, None, page, MLA_KV_RANK), page_map(g, u)) for g, u in slots]
    in_specs += [pl.BlockSpec((None, None, MLA_ROPE, page), page_map(g, u)) for g, u in slots]
    grid_spec = pltpu.PrefetchScalarGridSpec(
        num_scalar_prefetch=1,
        grid=(nseq // group, npg),
        in_specs=in_specs,
        out_specs=pl.BlockSpec((gt, MLA_HEADS * MLA_V), lambda b, j, pt: (b, 0)),
        scratch_shapes=[pltpu.VMEM((group, nrow, MLA_KV_RANK), F32),
                        pltpu.VMEM((group, nrow, MLA_ROPE), F32),
                        pltpu.VMEM((group, pages * page, MLA_KV_RANK), BF16),
                        pltpu.VMEM((group, MLA_ROPE, pages * page), BF16),
                        pltpu.VMEM((group, LANE, MLA_KV_RANK), F32),
                        pltpu.VMEM((group, LANE, MLA_ROPE), F32),
                        pltpu.VMEM((group, nrow, 1), F32),
                        pltpu.VMEM((group, nrow, 1), F32),
                        pltpu.VMEM((group, nrow, MLA_KV_RANK), F32)])
    n_in = len(slots)
    return pl.pallas_call(
        kern,
        out_shape=jax.ShapeDtypeStruct((nseq * ts, MLA_HEADS * MLA_V), F32),
        grid_spec=grid_spec,
        compiler_params=_params("arbitrary", "arbitrary"),
        name="attn_sample",
    )(page_table, ql, qr, lat, kr, wuv, *([cache_lat] * n_in), *([cache_krt] * n_in))


def _mamba_kernel(z_ref, xbc_ref, misc_ref, cbuf_ref, s0_ref, cw_ref, cbias_ref, dtb_ref, alog_ref, dpar_ref,
                  nw_ref, *rest, L, nc, group, has_stack):
    y_ref, sfin_ref, xp_ref, s_ref, ysc_ref = rest[1:] if has_stack else rest
    c = pl.program_id(1)
    pad = 8
    cw = cw_ref[...]
    cbias = cbias_ref[...]
    dtb = dtb_ref[...]
    neg_a = -jnp.exp(alog_ref[...])
    dpar = dpar_ref[...]
    nw = nw_ref[...]
    ri = lax.broadcasted_iota(jnp.int32, (L, L), 0)
    ci = lax.broadcasted_iota(jnp.int32, (L, L), 1)
    causal = ri >= ci
    hg = SSM_HEADS // SSM_GROUPS
    gw = SSM_INNER // SSM_GROUPS

    @pl.when(c == 0)
    def _():
        xp_ref[:, pad - 3:pad, :] = cbuf_ref[...]
        s_ref[...] = s0_ref[...]

    for q in range(group):
        r0 = q * L
        x = xbc_ref[r0:r0 + L, :]
        xp_ref[q, pad:pad + L, :] = x
        y = (cbias + cw[3:4] * x + cw[2:3] * xp_ref[q, pad - 1:pad - 1 + L, :]
             + cw[1:2] * xp_ref[q, pad - 2:pad - 2 + L, :] + cw[0:1] * xp_ref[q, pad - 3:pad - 3 + L, :])
        tail = xp_ref[q, pad + L - 3:pad + L, :]
        xp_ref[q, pad - 3:pad, :] = tail
        xa = _silu(y)
        bm = xa[:, SSM_INNER:SSM_INNER + SSM_GROUPS * SSM_STATE]
        cm = xa[:, SSM_INNER + SSM_GROUPS * SSM_STATE:]
        dt = jax.nn.softplus(misc_ref[r0:r0 + L, LANE:2 * LANE] + dtb)
        da = dt * neg_a
        acs = _dot(causal.astype(F32), da, HI)
        acs_t = acs.T
        eacs = jnp.exp(acs)
        last = acs[L - 1:L, :]
        dte = jnp.exp(last - acs)
        elast = jnp.exp(last)
        for g in range(SSM_GROUPS):
            bg = bm[:, g * SSM_STATE:(g + 1) * SSM_STATE].astype(BF16)
            cg = cm[:, g * SSM_STATE:(g + 1) * SSM_STATE].astype(BF16)
            cbm = _dot_nt(cg, bg)
            for hh in range(hg):
                h = g * hg + hh
                seg = jnp.where(causal, jnp.exp(acs[:, h:h + 1] - acs_t[h:h + 1, :]), 0.0)
                xh = xa[:, h * SSM_HEAD_DIM:(h + 1) * SSM_HEAD_DIM]
                xdt = xh * dt[:, h:h + 1]
                s_old = s_ref[q, h]
                yh = (_dot((cbm * seg).astype(BF16), xdt.astype(BF16))
                      + eacs[:, h:h + 1] * _dot_nt(cg, s_old.astype(BF16)) + dpar[:, h:h + 1] * xh)
                ysc_ref[r0:r0 + L, h * SSM_HEAD_DIM:(h + 1) * SSM_HEAD_DIM] = yh
                s_ref[q, h] = elast[:, h:h + 1] * s_old + _dot_tn((xdt * dte[:, h:h + 1]).astype(BF16), bg)
        yv = ysc_ref[r0:r0 + L, :] * _silu(z_ref[r0:r0 + L, :])
        for g in range(SSM_GROUPS):
            yg = yv[:, g * gw:(g + 1) * gw]
            y_ref[r0:r0 + L, g * gw:(g + 1) * gw] = _rms(yg, nw[:, g * gw:(g + 1) * gw])

    @pl.when(c == nc - 1)
    def _():
        sfin_ref[...] = s_ref[...]


def _stacked_state(state_blk, nseq, stack, layer, depth, n_in):
    shape = jax.ShapeDtypeStruct((depth, nseq) + state_blk[1:], F32)
    spec = pl.BlockSpec((None,) + state_blk, lambda b, c: (layer, b) + (0,) * (len(state_blk) - 1))
    if stack is None:
        return shape, spec, [], [], {}
    return shape, spec, [stack], [pl.BlockSpec(memory_space=pl.ANY)], {n_in: 1}


def _mamba(p, cbuf, s0, cw, cbias, dtb, alog, dpar, nw, row0, nseq, t, group_pref=4, stack=None, layer=0,
           depth=None):
    L = min(SSM_CHUNK, t)
    nc = t // L
    group = _tile(nseq, group_pref, 1) if nc == 1 else 1
    gl = group * L
    assert row0 % gl == 0
    blk0 = row0 // gl
    vec = lambda w: pl.BlockSpec((1, w), lambda b, c: (0, 0))
    state = (group, SSM_HEADS, SSM_HEAD_DIM, SSM_STATE)
    if depth is None:
        st_shape = jax.ShapeDtypeStruct((nseq,) + state[1:], F32)
        st_spec, extra, extra_specs, aliases = pl.BlockSpec(state, lambda b, c: (b, 0, 0, 0)), [], [], {}
    else:
        st_shape, st_spec, extra, extra_specs, aliases = _stacked_state(state, nseq, stack, layer, depth, 11)
    kern = functools.partial(_mamba_kernel, L=L, nc=nc, group=group, has_stack=bool(extra))
    return pl.pallas_call(
        kern,
        out_shape=[jax.ShapeDtypeStruct((nseq * t, SSM_INNER), F32), st_shape],
        input_output_aliases=aliases,
        grid=(nseq // group, nc),
        in_specs=[pl.BlockSpec((gl, SSM_INNER), lambda b, c: (blk0 + b * nc + c, COL_Z // SSM_INNER)),
                  pl.BlockSpec((gl, SSM_CONV_DIM), lambda b, c: (blk0 + b * nc + c, COL_XBC // SSM_CONV_DIM)),
                  pl.BlockSpec((gl, MISC_W), lambda b, c: (blk0 + b * nc + c, COL_MISC // MISC_W)),
                  pl.BlockSpec((group, SSM_CONV - 1, SSM_CONV_DIM), lambda b, c: (b, 0, 0)),
                  pl.BlockSpec(state, lambda b, c: (b, 0, 0, 0)),
                  pl.BlockSpec((SSM_CONV, SSM_CONV_DIM), lambda b, c: (0, 0)),
                  vec(SSM_CONV_DIM), vec(LANE), vec(LANE), vec(LANE), vec(SSM_INNER)] + extra_specs,
        out_specs=[pl.BlockSpec((gl, SSM_INNER), lambda b, c: (b * nc + c, 0)), st_spec],
        scratch_shapes=[pltpu.VMEM((group, L + 8, SSM_CONV_DIM), F32),
                        pltpu.VMEM(state, F32),
                        pltpu.VMEM((gl, SSM_INNER), F32)],
        compiler_params=_params("arbitrary", "arbitrary"),
        name="ssd_scan",
    )(p, p, p, cbuf, s0, cw, cbias, dtb, alog, dpar, nw, *extra)


def _gla_kernel(gq_ref, gk_ref, gv_ref, og_ref, misc_ref, s0_ref, wg_ref, bg_ref, gn_ref, *rest, L, ls, nc,
                has_stack):
    o_ref, sfin_ref, st_ref, gl_ref = rest[1:] if has_stack else rest
    c = pl.program_id(1)

    @pl.when(c == 0)
    def _():
        for h in range(GLA_HEADS):
            st_ref[h] = s0_ref[h].T

    gl_ref[...] = jax.nn.log_sigmoid(_dot(misc_ref[:, LANE:2 * LANE], wg_ref[...], HI) + bg_ref[...]) \
        * (1.0 / GLA_GATE_NORMALIZER)
    ri = lax.broadcasted_iota(jnp.int32, (ls, ls), 0)
    ci = lax.broadcasted_iota(jnp.int32, (ls, ls), 1)
    tril = (ri >= ci).astype(F32)
    rows = lax.broadcasted_iota(jnp.int32, (ls, GLA_DK), 0)
    gn = gn_ref[...]

    def sub_chunk(sidx, carry):
        r0 = pl.multiple_of(sidx * ls, ls)
        for h in range(GLA_HEADS):
            q = gq_ref[pl.ds(r0, ls), h * GLA_DK:(h + 1) * GLA_DK] * (GLA_DK ** -0.5)
            k = gk_ref[pl.ds(r0, ls), h * GLA_DK:(h + 1) * GLA_DK]
            v = gv_ref[pl.ds(r0, ls), h * GLA_DV:(h + 1) * GLA_DV]
            g = gl_ref[pl.ds(r0, ls), h * GLA_DK:(h + 1) * GLA_DK]
            bcs = _dot(tril, g, HI)
            st = st_ref[h]
            o = _dot_nt((q * jnp.exp(bcs)).astype(BF16), st.astype(BF16))
            for j in range(ls):
                dj = jnp.where(rows >= j, bcs - bcs[j:j + 1, :], NEG_INF)
                w = jnp.sum(q * k[j:j + 1, :] * jnp.exp(dj), axis=-1, keepdims=True)
                o = o + w * v[j:j + 1, :]
            lastb = bcs[ls - 1:ls, :]
            st_ref[h] = st * jnp.exp(lastb) + _dot_tn(v.astype(BF16), (k * jnp.exp(lastb - bcs)).astype(BF16))
            o = o * lax.rsqrt(jnp.mean(o * o, axis=-1, keepdims=True) + EPS) * gn
            o = o * _silu(og_ref[pl.ds(r0, ls), h * GLA_DV:(h + 1) * GLA_DV])
            o_ref[pl.ds(r0, ls), h * GLA_DV:(h + 1) * GLA_DV] = o
        return carry

    lax.fori_loop(0, L // ls, sub_chunk, 0)

    @pl.when(c == nc - 1)
    def _():
        for h in range(GLA_HEADS):
            sfin_ref[h] = st_ref[h].T


def _gla(p, s0, wg, bg, gn, row0, nseq, t, stack=None, layer=0, depth=None):
    L = min(128, t)
    ls = min(GLA_SUB, L)
    nc = t // L
    blk0 = row0 // L
    rowblk = lambda w, col: pl.BlockSpec((L, w), lambda b, c: (blk0 + b * nc + c, col // w))
    state = (None, GLA_HEADS, GLA_DK, GLA_DV)
    if depth is None:
        st_shape = jax.ShapeDtypeStruct((nseq,) + state[1:], F32)
        st_spec, extra, extra_specs, aliases = pl.BlockSpec(state, lambda b, c: (b, 0, 0, 0)), [], [], {}
    else:
        st_shape, st_spec, extra, extra_specs, aliases = _stacked_state(state, nseq, stack, layer, depth, 9)
    kern = functools.partial(_gla_kernel, L=L, ls=ls, nc=nc, has_stack=bool(extra))
    return pl.pallas_call(
        kern,
        out_shape=[jax.ShapeDtypeStruct((nseq * t, GLA_VALUE), F32), st_shape],
        input_output_aliases=aliases,
        grid=(nseq, nc),
        in_specs=[rowblk(GLA_KEY, COL_GQ), rowblk(GLA_KEY, COL_GK), rowblk(GLA_VALUE, COL_GV),
                  rowblk(GLA_VALUE, COL_OG), rowblk(MISC_W, COL_MISC),
                  pl.BlockSpec((None, GLA_HEADS, GLA_DK, GLA_DV), lambda b, c: (b, 0, 0, 0)),
                  pl.BlockSpec((LANE, GLA_KEY), lambda b, c: (0, 0)),
                  pl.BlockSpec((1, GLA_KEY), lambda b, c: (0, 0)),
                  pl.BlockSpec((1, GLA_DV), lambda b, c: (0, 0))] + extra_specs,
        out_specs=[pl.BlockSpec((L, GLA_VALUE), lambda b, c: (b * nc + c, 0)), st_spec],
        scratch_shapes=[pltpu.VMEM((GLA_HEADS, GLA_DV, GLA_DK), F32),
                        pltpu.VMEM((L, GLA_KEY), F32)],
        compiler_params=_params("arbitrary", "arbitrary"),
        name="gla_scan",
    )(p, p, p, p, p, s0, wg, bg, gn, *extra)


def _merge_kernel(oa_ref, ob_ref, oc_ref, ga_ref, gb_ref, gc_ref, wb_ref, o_ref):
    acc = jax.nn.sigmoid(ga_ref[...]) * _dot(oa_ref[...].astype(BF16), wb_ref[0])
    acc += jax.nn.sigmoid(gb_ref[...]) * _dot(ob_ref[...].astype(BF16), wb_ref[1])
    acc += jax.nn.sigmoid(gc_ref[...]) * _dot(oc_ref[...].astype(BF16), wb_ref[2])
    o_ref[...] = acc.astype(BF16)


def _merge(oa, ob, oc, p, wb, layer, tm_pref=512, tn=512):
    rows = oa.shape[0]
    tm = _tile(rows, tm_pref, 16)
    nj = D_MODEL // tn
    br = lambda: pl.BlockSpec((tm, BRANCH_W), lambda j, i: (i, 0))
    gate = lambda n: pl.BlockSpec((tm, tn), lambda j, i: (i, (COL_GATES + n * D_MODEL) // tn + j))
    return pl.pallas_call(
        _merge_kernel,
        out_shape=jax.ShapeDtypeStruct((rows, D_MODEL), BF16),
        grid=(nj, rows // tm),
        in_specs=[br(), br(), br(), gate(0), gate(1), gate(2),
                  pl.BlockSpec((None, 3, BRANCH_W, tn), lambda j, i: (layer, 0, 0, j))],
        out_specs=pl.BlockSpec((tm, tn), lambda j, i: (i, j)),
        compiler_params=_params("arbitrary", "arbitrary"),
        name="branch_merge",
    )(oa, ob, oc, p, p, p, wb)


def _out_proj_kernel(m_ref, w_ref, x_ref, o_ref):
    o_ref[...] = x_ref[...] + _dot(m_ref[...], w_ref[...])


def _out_proj(m, w, x, layer, tm_pref=512, tn=512):
    rows, k = m.shape
    n = w.shape[2]
    tm = _tile(rows, tm_pref, 16)
    return pl.pallas_call(
        _out_proj_kernel,
        out_shape=jax.ShapeDtypeStruct((rows, n), F32),
        grid=(n // tn, rows // tm),
        in_specs=[pl.BlockSpec((tm, k), lambda j, i: (i, 0)),
                  pl.BlockSpec((None, k, tn), lambda j, i: (layer, 0, j)),
                  pl.BlockSpec((tm, tn), lambda j, i: (i, j))],
        out_specs=pl.BlockSpec((tm, tn), lambda j, i: (i, j)),
        compiler_params=_params("arbitrary", "arbitrary"),
        name="out_proj",
    )(m, w, x)


def _top16(s):
    n = s.shape[0]
    iota = lax.broadcasted_iota(jnp.int32, s.shape, 0)
    rank = jnp.full(s.shape, PEER_TOPK, jnp.int32)
    vals = []
    for r in range(PEER_TOPK):
        m = jnp.max(s, axis=0, keepdims=True)
        pos = jnp.min(jnp.where(s == m, iota, n), axis=0, keepdims=True)
        hit = iota == pos
        rank = jnp.where(hit, r, rank)
        s = jnp.where(hit, NEG_INF, s)
        vals.append(m)
    return rank, vals


_CAND_ROWS = [PEER_TOPK // (r1 + 1) for r1 in range(PEER_TOPK)]
_CAND_N = sum(_CAND_ROWS)
_CAND_PAD = -(-_CAND_N // 8) * 8


def _peer_route_kernel(x_ref, g_ref, wqt_ref, sk_ref, hq_ref, rank2_ref, cnt_ref, a_ref, b_ref, cand_ref):
    hq = _rms(x_ref[...], g_ref[...]).astype(BF16)
    hq_ref[...] = hq
    qt = _dot_nt(wqt_ref[...], hq)
    t = qt.shape[1]
    iota_c = lax.broadcasted_iota(jnp.int32, (_CAND_PAD, t), 0)
    cand_ref[_CAND_N:_CAND_PAD, :] = jnp.full((_CAND_PAD - _CAND_N, t), NEG_INF, F32)
    for h in range(PEER_HEADS):
        base = h * 2 * PEER_HALF
        s1 = _dot(sk_ref[0, h], qt[base:base + PEER_HALF].astype(BF16))
        s2 = _dot(sk_ref[1, h], qt[base + PEER_HALF:base + 2 * PEER_HALF].astype(BF16))
        rank1, v1 = _top16(s1)
        rank2, v2 = _top16(s2)
        v2s = jnp.concatenate(v2, axis=0)
        off = 0
        for r1, n in enumerate(_CAND_ROWS):
            cand_ref[off:off + n, :] = v1[r1] + v2s[0:n]
            off += n
        cand = cand_ref[...]
        top = v1[0] + v2[0]
        picked = jnp.zeros((_CAND_PAD, t), F32)
        z = jnp.zeros((1, t), F32)
        for _ in range(PEER_TOPK):
            m = jnp.max(cand, axis=0, keepdims=True)
            pos = jnp.min(jnp.where(cand == m, iota_c, _CAND_PAD), axis=0, keepdims=True)
            hit = iota_c == pos
            z = z + jnp.exp(m - top)
            picked = jnp.where(hit, 1.0, picked)
            cand = jnp.where(hit, NEG_INF, cand)
        cntrow = jnp.zeros(s1.shape, F32)
        off = 0
        for r1, n in enumerate(_CAND_ROWS):
            cnt = jnp.sum(picked[off:off + n], axis=0, keepdims=True)
            cntrow = jnp.where(rank1 == r1, cnt, cntrow)
            off += n
        rank2_ref[h] = rank2.astype(F32).astype(BF16)
        cnt_ref[h] = cntrow
        a_ref[h] = jnp.where(rank1 < PEER_TOPK, jnp.exp(s1 - v1[0]), 0.0) / z
        b_ref[h] = jnp.where(rank2 < PEER_TOPK, jnp.exp(s2 - v2[0]), 0.0).astype(BF16)


def _peer_route(x, g, wqt, sk, layer, tt):
    rows, d = x.shape
    tab = jax.ShapeDtypeStruct((PEER_HEADS, PEER_NKEYS, rows), F32)
    tab16 = jax.ShapeDtypeStruct((PEER_HEADS, PEER_NKEYS, rows), BF16)
    tab_spec = pl.BlockSpec((PEER_HEADS, PEER_NKEYS, tt), lambda i: (0, 0, i))
    return pl.pallas_call(
        _peer_route_kernel,
        out_shape=[jax.ShapeDtypeStruct((rows, d), BF16), tab16, tab, tab, tab16],
        grid=(rows // tt,),
        in_specs=[pl.BlockSpec((tt, d), lambda i: (i, 0)),
                  pl.BlockSpec((1, d), lambda i: (0, 0)),
                  pl.BlockSpec((None,) + wqt.shape[1:], lambda i: (layer, 0, 0)),
                  pl.BlockSpec(sk.shape, lambda i: (0, 0, 0, 0))],
        out_specs=[pl.BlockSpec((tt, d), lambda i: (i, 0)), tab_spec, tab_spec, tab_spec, tab_spec],
        scratch_shapes=[pltpu.VMEM((_CAND_PAD, tt), F32)],
        compiler_params=_params("arbitrary"),
        name="peer_route",
    )(x, g, wqt, sk)


def _peer_dense_kernel(hq_ref, x_ref, u_ref, v_ref, rank2_ref, cnt_ref, a_ref, b_ref, o_ref, acc_ref, wa0_ref,
                       wa1_ref, *, eb, ne, half):
    e = pl.program_id(1)
    wa_refs = (wa0_ref, wa1_ref)

    @pl.when(e == 0)
    def _():
        acc_ref[...] = jnp.zeros(acc_ref.shape, F32)
        wa1_ref[...] = jnp.zeros(wa1_ref.shape, BF16)

    def value_matmul(wa_prev_ref):
        acc_ref[...] += _dot_tn(wa_prev_ref[...], v_ref[...])

    def build(wa_next_ref):
        nsub = half // PEER_NKEYS
        for hf in range(eb // half):
            act = _dot_nt(u_ref[hf * half:(hf + 1) * half, :], hq_ref[...])
            act = 0.5 * act * (1.0 + lax.erf(act * (2.0 ** -0.5)))
            for ii in range(nsub):
                i = hf * nsub + ii
                w = None
                for h in range(PEER_HEADS):
                    c = cnt_ref[h, i:i + 1, :].astype(BF16)
                    av = a_ref[h, i:i + 1, :].astype(BF16)
                    term = av * jnp.where(rank2_ref[h] < c, b_ref[h], jnp.zeros((), BF16))
                    w = term if w is None else w + term
                r0 = hf * half + ii * PEER_NKEYS
                wa_next_ref[r0:r0 + PEER_NKEYS, :] = w * act[ii * PEER_NKEYS:(ii + 1) * PEER_NKEYS].astype(BF16)

    for parity in range(2):
        @pl.when(jnp.logical_and(e < ne, e % 2 == parity))
        def _(parity=parity):
            value_matmul(wa_refs[1 - parity])
            build(wa_refs[parity])

    @pl.when(e == ne)
    def _():
        value_matmul(wa_refs[(ne - 1) % 2])
        o_ref[...] = x_ref[...] + acc_ref[...]


def _peer_dense(hq, x, u, v, rank2, cntrow, a, b, layer, tt, eb=1024, half=256):
    rows, d = x.shape
    n_exp = u.shape[1]
    ne = n_exp // eb
    kern = functools.partial(_peer_dense_kernel, eb=eb, ne=ne, half=half)
    cur = lambda e: jnp.minimum(e, ne - 1)
    prev = lambda e: jnp.maximum(e - 1, 0)
    tab_spec = pl.BlockSpec((PEER_HEADS, PEER_NKEYS, tt), lambda i, e: (0, 0, i))
    row_spec = pl.BlockSpec((PEER_HEADS, eb // PEER_NKEYS, tt), lambda i, e: (0, cur(e), i))
    return pl.pallas_call(
        kern,
        out_shape=jax.ShapeDtypeStruct((rows, d), F32),
        grid=(rows // tt, ne + 1),
        in_specs=[pl.BlockSpec((tt, d), lambda i, e: (i, 0)),
                  pl.BlockSpec((tt, d), lambda i, e: (i, 0)),
                  pl.BlockSpec((None, eb, d), lambda i, e: (layer, cur(e), 0)),
                  pl.BlockSpec((None, eb, d), lambda i, e: (layer, prev(e), 0)),
                  tab_spec, row_spec, row_spec, tab_spec],
        out_specs=pl.BlockSpec((tt, d), lambda i, e: (i, 0)),
        scratch_shapes=[pltpu.VMEM((tt, d), F32), pltpu.VMEM((eb, tt), BF16), pltpu.VMEM((eb, tt), BF16)],
        compiler_params=_params("arbitrary", "arbitrary"),
        name="peer_dense",
    )(hq, x, u, v, rank2, cntrow, a, b)


def _final_norm_kernel(x_ref, g_ref, o_ref):
    o_ref[...] = _rms(x_ref[...], g_ref[...])


def _final_norm(x, g, tm_pref=512):
    rows, d = x.shape
    tm = _tile(rows, tm_pref)
    return pl.pallas_call(
        _final_norm_kernel,
        out_shape=jax.ShapeDtypeStruct((rows, d), F32),
        grid=(rows // tm,),
        in_specs=[pl.BlockSpec((tm, d), lambda i: (i, 0)), pl.BlockSpec((1, d), lambda i: (0, 0))],
        out_specs=pl.BlockSpec((tm, d), lambda i: (i, 0)),
        compiler_params=_params("arbitrary"),
        name="final_norm",
    )(x, g)


def _pad_lanes(v, width=LANE):
    return jnp.pad(v, [(0, 0)] * (v.ndim - 1) + [(0, width - v.shape[-1])])


def _rot_half(w):
    half = w.shape[-1] // 2
    return jnp.concatenate([w[..., half:], w[..., :half]], axis=-1)


def _layout_w_in(w_in):
    widths = (MLA_Q_RANK, MLA_KV_RANK, MLA_ROPE, SSM_INNER, SSM_CONV_DIM, SSM_HEADS,
              GLA_KEY, GLA_KEY, GLA_VALUE, GLA_GATE_RANK, GLA_VALUE, 3 * D_MODEL)
    offs = [0]
    for w in widths:
        offs.append(offs[-1] + w)
    cq, ckv, kr, z, xbc, dt, gq, gk, gv, glr, og, gates = [w_in[..., offs[i]:offs[i + 1]] for i in range(len(widths))]
    zero = jnp.zeros(w_in.shape[:-1] + (MISC_W - 2 * MLA_ROPE - SSM_HEADS - GLA_GATE_RANK,), w_in.dtype)
    out = jnp.concatenate([gates, z, og, gv, xbc, gq, gk, cq, ckv, kr, _rot_half(kr), dt, glr, zero], axis=-1)
    return out.astype(BF16)


def _rope_tables(pos):
    half = MLA_ROPE // 2
    inv = ROPE_BASE ** (-jnp.arange(half, dtype=F32) / half)
    ang = pos.astype(F32)[:, None] * inv[None, :]
    cos, sin = jnp.cos(ang), jnp.sin(ang)
    cos_t = jnp.tile(jnp.concatenate([cos, cos], axis=-1), (1, MLA_HEADS))
    sin_t = jnp.tile(jnp.concatenate([-sin, sin], axis=-1), (1, MLA_HEADS))
    return cos_t, sin_t


def kernel(x_prompt, x_sample, cache_mla_latent, cache_mla_krope, page_table, state_ssm_conv, state_ssm, state_gla, norm_mix, w_in, mla_q_norm, mla_w_uq, mla_kv_norm, mla_w_uk, mla_w_uv, ssm_conv_w, ssm_conv_b, ssm_dt_bias, ssm_a_log, ssm_d, ssm_norm, gla_gate_w, gla_gate_b, gla_norm, w_branch, w_out, norm_ffn, peer_wq, peer_subkeys, peer_u, peer_v, norm_final):
    bp, tp, d = x_prompt.shape
    bs, ts, _ = x_sample.shape
    depth = w_in.shape[0]
    n_pages = page_table.shape[1]
    past_len = n_pages * cache_mla_latent.shape[2]
    rows_p = bp * tp
    rows_s = bs * ts
    rows = rows_p + rows_s

    x = jnp.concatenate([x_prompt.reshape(rows_p, d), x_sample.reshape(rows_s, d)], axis=0)
    pos = jnp.concatenate([jnp.tile(jnp.arange(tp), bp), jnp.tile(past_len + jnp.arange(ts), bs)])
    cos_t, sin_t = _rope_tables(pos)
    w_in_l = _layout_w_in(w_in)
    uq_nope = mla_w_uq[..., :MLA_NOPE].reshape(depth, MLA_Q_RANK, MLA_HEADS * MLA_NOPE)
    uq_rope = mla_w_uq[..., MLA_NOPE:]
    wq_all = jnp.concatenate([uq_nope,
                              uq_rope.reshape(depth, MLA_Q_RANK, MLA_HEADS * MLA_ROPE),
                              _rot_half(uq_rope).reshape(depth, MLA_Q_RANK, MLA_HEADS * MLA_ROPE)],
                             axis=-1).astype(BF16)
    wuk_t = jnp.transpose(mla_w_uk, (0, 2, 3, 1)).astype(BF16)
    wuv = jnp.transpose(mla_w_uv, (0, 2, 1, 3)).astype(BF16)
    wuvt = jnp.transpose(mla_w_uv, (0, 2, 3, 1)).astype(BF16)
    dtb = _pad_lanes(ssm_dt_bias)
    alog = _pad_lanes(ssm_a_log)
    dpar = _pad_lanes(ssm_d)
    wg = jnp.zeros((depth, LANE, GLA_KEY), F32).at[:, SSM_HEADS:SSM_HEADS + GLA_GATE_RANK, :].set(gla_gate_w)
    wb = w_branch.astype(BF16)
    wo = w_out.astype(BF16)
    wqt = jnp.swapaxes(peer_wq, 1, 2).astype(BF16)
    sk = peer_subkeys.astype(BF16)
    ub = peer_u.astype(BF16)
    vb = peer_v.astype(BF16)
    cache_krt = jnp.swapaxes(cache_mla_krope, 2, 3)
    conv0 = jnp.zeros((bp, SSM_CONV - 1, SSM_CONV_DIM), F32)
    ssm0 = jnp.zeros((bp, SSM_HEADS, SSM_HEAD_DIM, SSM_STATE), F32)
    gla0 = jnp.zeros((bp, GLA_HEADS, GLA_DK, GLA_DV), F32)
    tt_route = _tile(rows, 256, LANE)
    tt_dense = _tile(rows, 512, LANE)

    new_p = [[], [], [], [], []]
    new_s = [[], [], []]
    ssm_stack = gla_stack = None
    for l in range(depth):
        p = _norm_matmul(x, norm_mix[l][None], w_in_l, l)
        ql, qr, lat, kr = _mla_prep(p, cos_t, sin_t, mla_q_norm[l][None], mla_kv_norm[l][None], wq_all[l], wuk_t[l])
        oa_p = _attn_prompt(ql, qr, lat, kr, wuvt[l], bp, tp)
        oa_s = _attn_sample(ql, qr, lat, kr, cache_mla_latent, cache_krt, page_table, l, wuv[l],
                            rows_p, bs, ts)
        ssm_args = (ssm_conv_w[l], ssm_conv_b[l][None], dtb[l][None], alog[l][None], dpar[l][None],
                    ssm_norm[l][None])
        ob_p, ssm_p = _mamba(p, conv0, ssm0, *ssm_args, 0, bp, tp)
        ob_s, ssm_stack = _mamba(p, state_ssm_conv[l], state_ssm[l], *ssm_args, rows_p, bs, ts,
                                 stack=ssm_stack, layer=l, depth=depth)
        gla_args = (wg[l], gla_gate_b[l][None], gla_norm[l][None])
        oc_p, gla_p = _gla(p, gla0, *gla_args, 0, bp, tp)
        oc_s, gla_stack = _gla(p, state_gla[l], *gla_args, rows_p, bs, ts, stack=gla_stack, layer=l, depth=depth)
        merged = _merge(jnp.concatenate([oa_p, oa_s]), jnp.concatenate([ob_p, ob_s]),
                        jnp.concatenate([oc_p, oc_s]), p, wb, l)
        x = _out_proj(merged, wo, x, l)
        hq, rank2, cntrow, ga, gb = _peer_route(x, norm_ffn[l][None], wqt, sk[l], l, tt_route)
        x = _peer_dense(hq, x, ub, vb, rank2, cntrow, ga, gb, l, tt_dense)

        xbc = p[:, COL_XBC:COL_XBC + SSM_CONV_DIM]
        conv_p = xbc[:rows_p].reshape(bp, tp, SSM_CONV_DIM)[:, tp - (SSM_CONV - 1):]
        conv_s = xbc[rows_p:].reshape(bs, ts, SSM_CONV_DIM)[:, ts - (SSM_CONV - 1):]
        for lst, a in zip(new_p, (lat[:rows_p].reshape(bp, tp, MLA_KV_RANK), kr[:rows_p].reshape(bp, tp, MLA_ROPE),
                                  conv_p, ssm_p, gla_p)):
            lst.append(a)
        for lst, a in zip(new_s, (lat[rows_p:].reshape(bs, ts, MLA_KV_RANK), kr[rows_p:].reshape(bs, ts, MLA_ROPE),
                                  conv_s)):
            lst.append(a)

    y = _final_norm(x, norm_final[None])
    y_prompt = y[:rows_p].reshape(bp, tp, d)
    y_sample = y[rows_p:].reshape(bs, ts, d)
    outs_p = [jnp.stack(a) for a in new_p]
    outs_s = [jnp.stack(a) for a in new_s]
    return (y_prompt, y_sample, *outs_p, *outs_s, ssm_stack, gla_stack)
```
